```python
import math
import jax, jax.numpy as jnp
from jax import lax
import numpy as np

D_MODEL = 1024
BATCH = 8
SEQ = 4096
DEPTH = 1
DEC_BATCH = 4
DEC_SEQ = 4096
PAST_LEN = 128

HYENA_WIDTH = D_MODEL // 2
RET_WIDTH = D_MODEL - HYENA_WIDTH
HYENA_ORDER = 2
HYENA_COLS = (HYENA_ORDER + 1) * HYENA_WIDTH
FILTER_BANDS = 16
FILTER_EMB = 1 + 2 * FILTER_BANDS
FILTER_HIDDEN = 64
DECAY_TARGET = 1e-2
FAST_DECAY_PCT = 0.3
SLOW_DECAY_PCT = 1.5
RET_HEADS = 8
RET_HEAD_DIM = RET_WIDTH // RET_HEADS
RET_COLS = 4 * RET_WIDTH
CHUNK = 128
ROPE_BASE = 10000.0
DECAY_OFFSET_FWD = 5.0
DECAY_OFFSET_BWD = 5.5
D_FF = ((8 * D_MODEL // 3 + 127) // 128) * 128
IN_COLS = HYENA_COLS + RET_COLS
EPS = 1e-6

kernel_name = 'hyena_retention_hybrid_encoder'

F32 = jnp.float32


def rmsnorm(x, g):
    xf = x.astype(F32)
    y = xf * lax.rsqrt(jnp.mean(xf * xf, axis=-1, keepdims=True) + EPS) * g.astype(F32)
    return y.astype(x.dtype)


def dwconv3(u, w, b):
    up = jnp.pad(u, ((0, 0), (1, 1), (0, 0)))
    return up[:, :-2] * w[0] + up[:, 1:-1] * w[1] + up[:, 2:] * w[2] + b


def hyena_filter_spectra(L, w1, b1, w2, b2, w3, b3, freq, w4):
    w1, b1, w2, b2, w3, b3, freq, w4 = (a.astype(F32) for a in (w1, b1, w2, b2, w3, b3, freq, w4))
    pos = jnp.arange(L, dtype=F32)[:, None]
    t = jnp.linspace(0.0, 1.0, L, dtype=F32)[:, None]
    bands = jnp.linspace(1e-4, FILTER_BANDS - 1, FILTER_BANDS, dtype=F32)[None, :]
    ang = (2.0 * math.pi / L) * pos * bands
    z = jnp.concatenate([t, jnp.cos(ang), -jnp.sin(ang)], axis=-1)
    h = jnp.sin(freq * (z @ w1 + b1))
    h = jnp.sin(freq * (h @ w2 + b2))
    h = jnp.sin(freq * (h @ w3 + b3))
    h = (h @ w4).reshape(L, 2, HYENA_ORDER, HYENA_WIDTH)
    deltas = jnp.abs(jnp.linspace(math.log(DECAY_TARGET) / FAST_DECAY_PCT,
                                  math.log(DECAY_TARGET) / SLOW_DECAY_PCT,
                                  HYENA_WIDTH, dtype=F32))
    h = h * jnp.exp(-t * deltas)[:, None, None, :]
    h_fwd, h_bwd = h[:, 0], h[:, 1]
    two_sided = jnp.concatenate([h_fwd, jnp.zeros_like(h_fwd[:1]), h_bwd[1:][::-1]], axis=0)
    return jnp.fft.rfft(two_sided, axis=0)


def long_conv(u, spec, bias):
    L = u.shape[1]
    uf32 = u.astype(F32)
    uf = jnp.fft.rfft(uf32, n=2 * L, axis=1)
    y = jnp.fft.irfft(uf * spec[None], n=2 * L, axis=1)[:, :L]
    return (y + uf32 * bias.astype(F32)).astype(u.dtype)


def hyena_mixer(p, conv_w, conv_b, spec, bias, out_g):
    p = dwconv3(p, conv_w, conv_b)
    parts = jnp.split(p, HYENA_ORDER + 1, axis=-1)
    z = parts[0]
    for o in range(HYENA_ORDER):
        z = parts[o + 1] * long_conv(z, spec[:, o], bias[o])
    return rmsnorm(z, out_g)


def rotary(x):
    L = x.shape[1]
    half = RET_HEAD_DIM // 2
    inv = ROPE_BASE ** (-jnp.arange(0, RET_HEAD_DIM, 2, dtype=F32) / RET_HEAD_DIM)
    ang = jnp.arange(L, dtype=F32)[:, None] * inv[None, :]
    cos = jnp.cos(ang)[None, :, None, :]
    sin = jnp.sin(ang)[None, :, None, :]
    x1, x2 = x[..., :half], x[..., half:]
    return jnp.concatenate([x1 * cos - x2 * sin, x2 * cos + x1 * sin], axis=-1)


def retention_decays(offset):
    return 1.0 - 2.0 ** (-offset - jnp.arange(RET_HEADS, dtype=F32))


def chunk_retention(q, k, v, gamma, include_diag):
    B, H, L, Dh = q.shape
    n = L // CHUNK
    q = q.reshape(B, H, n, CHUNK, Dh)
    k = k.reshape(B, H, n, CHUNK, Dh)
    v = v.reshape(B, H, n, CHUNK, Dh)
    log_g = jnp.log(gamma)
    idx = jnp.arange(CHUNK, dtype=F32)
    diff = idx[:, None] - idx[None, :]
    mask = (diff >= 0) if include_diag else (diff > 0)
    decay_in = jnp.where(mask[None], jnp.exp(jnp.where(mask, diff, 0.0)[None] * log_g[:, None, None]), 0.0)
    scores = jnp.einsum('bhncd,bhnsd->bhncs', q, k) * decay_in[None, :, None]
    intra = jnp.einsum('bhncs,bhnse->bhnce', scores, v)
    key_decay = jnp.exp((CHUNK - 1.0 - idx)[None, :] * log_g[:, None])
    query_decay = jnp.exp((idx + 1.0)[None, :] * log_g[:, None])
    chunk_decay = jnp.exp(CHUNK * log_g)[None, :, None, None]
    kv = jnp.einsum('bhncd,hc,bhnce->nbhde', k, key_decay, v)

    def step(state, kv_n):
        return state * chunk_decay + kv_n, state

    _, prev = lax.scan(step, jnp.zeros((B, H, Dh, Dh), F32), kv)
    cross = jnp.einsum('bhncd,nbhde->bhnce', q, prev) * query_decay[None, :, None, :, None]
    return (intra + cross).reshape(B, H, L, Dh)


def retention_mixer(p, gn_g, gn_b):
    B, L, _ = p.shape
    q, k, v, g = jnp.split(p.astype(F32), 4, axis=-1)
    heads = lambda a: a.reshape(B, L, RET_HEADS, RET_HEAD_DIM)
    q = rotary(heads(q)).transpose(0, 2, 1, 3)
    k = (rotary(heads(k)) * RET_HEAD_DIM ** -0.5).transpose(0, 2, 1, 3)
    v = heads(v).transpose(0, 2, 1, 3)
    o_fwd = chunk_retention(q, k, v, retention_decays(DECAY_OFFSET_FWD), True)
    o_bwd = chunk_retention(q[:, :, ::-1], k[:, :, ::-1], v[:, :, ::-1],
                            retention_decays(DECAY_OFFSET_BWD), False)[:, :, ::-1]
    o = o_fwd + o_bwd
    mu = jnp.mean(o, axis=-1, keepdims=True)
    oc = o - mu
    o = oc * lax.rsqrt(jnp.mean(oc * oc, axis=-1, keepdims=True) + EPS)
    o = o.transpose(0, 2, 1, 3).reshape(B, L, RET_WIDTH) * gn_g.astype(F32) + gn_b.astype(F32)
    return (o * jax.nn.silu(g)).astype(p.dtype)


def conv_ffn(x, w_up, conv_w, conv_b, w_down):
    u = dwconv3(x @ w_up, conv_w, conv_b)
    a, b = jnp.split(u, 2, axis=-1)
    return (jax.nn.gelu(a) * b) @ w_down


def trunk(x, prm):
    L = x.shape[1]
    for l in range(DEPTH):
        spec = hyena_filter_spectra(L, prm['filt_w1'][l], prm['filt_b1'][l], prm['filt_w2'][l],
                                    prm['filt_b2'][l], prm['filt_w3'][l], prm['filt_b3'][l],
                                    prm['filt_freq'][l], prm['filt_w4'][l])
        h = rmsnorm(x, prm['norm1_g'][l])
        p = h @ prm['w_in'][l]
        y_hy = hyena_mixer(p[..., :HYENA_COLS], prm['hy_conv_w'][l], prm['hy_conv_b'][l],
                           spec, prm['hy_bias'][l], prm['hy_out_g'][l])
        y_rt = retention_mixer(p[..., HYENA_COLS:], prm['ret_gn_g'][l], prm['ret_gn_b'][l])
        x = x + jnp.concatenate([y_hy, y_rt], axis=-1) @ prm['w_out'][l]
        x = x + conv_ffn(rmsnorm(x, prm['norm2_g'][l]), prm['w_up'][l], prm['ffn_conv_w'][l],
                         prm['ffn_conv_b'][l], prm['w_down'][l])
    return rmsnorm(x, prm['final_g'])


def setup_inputs(seed: int = 0) -> dict:
    key = jax.random.key(seed)
    ks = jax.random.split(key, 26)
    nrm = lambda k, shape, s: jax.random.normal(k, shape, F32) * s
    return {
        'x_prompt': nrm(ks[0], (BATCH, SEQ, D_MODEL), 1.0),
        'x_sample': nrm(ks[1], (DEC_BATCH, DEC_SEQ, D_MODEL), 1.0),
        'norm1_g': 1.0 + nrm(ks[2], (DEPTH, D_MODEL), 0.02),
        'w_in': nrm(ks[3], (DEPTH, D_MODEL, IN_COLS), D_MODEL ** -0.5),
        'hy_conv_w': nrm(ks[4], (DEPTH, 3, HYENA_COLS), 3 ** -0.5),
        'hy_conv_b': nrm(ks[5], (DEPTH, HYENA_COLS), 0.02),
        'filt_w1': nrm(ks[6], (DEPTH, FILTER_EMB, FILTER_HIDDEN), FILTER_EMB ** -0.5),
        'filt_b1': nrm(ks[7], (DEPTH, FILTER_HIDDEN), 0.1),
        'filt_w2': nrm(ks[8], (DEPTH, FILTER_HIDDEN, FILTER_HIDDEN), FILTER_HIDDEN ** -0.5),
        'filt_b2': nrm(ks[9], (DEPTH, FILTER_HIDDEN), 0.1),
        'filt_w3': nrm(ks[10], (DEPTH, FILTER_HIDDEN, FILTER_HIDDEN), FILTER_HIDDEN ** -0.5),
        'filt_b3': nrm(ks[11], (DEPTH, FILTER_HIDDEN), 0.1),
        'filt_freq': 1.0 + nrm(ks[12], (DEPTH, FILTER_HIDDEN), 0.1),
        'filt_w4': nrm(ks[13], (DEPTH, FILTER_HIDDEN, 2 * HYENA_ORDER * HYENA_WIDTH), 0.05 * FILTER_HIDDEN ** -0.5),
        'hy_bias': nrm(ks[14], (DEPTH, HYENA_ORDER, HYENA_WIDTH), 0.1),
        'hy_out_g': 1.0 + nrm(ks[15], (DEPTH, HYENA_WIDTH), 0.02),
        'ret_gn_g': 1.0 + nrm(ks[16], (DEPTH, RET_WIDTH), 0.02),
        'ret_gn_b': nrm(ks[17], (DEPTH, RET_WIDTH), 0.02),
        'w_out': nrm(ks[18], (DEPTH, D_MODEL, D_MODEL), D_MODEL ** -0.5),
        'norm2_g': 1.0 + nrm(ks[19], (DEPTH, D_MODEL), 0.02),
        'w_up': nrm(ks[20], (DEPTH, D_MODEL, 2 * D_FF), D_MODEL ** -0.5),
        'ffn_conv_w': nrm(ks[21], (DEPTH, 3, 2 * D_FF), 3 ** -0.5),
        'ffn_conv_b': nrm(ks[22], (DEPTH, 2 * D_FF), 0.02),
        'w_down': nrm(ks[23], (DEPTH, D_FF, D_MODEL), D_FF ** -0.5),
        'final_g': 1.0 + nrm(ks[24], (D_MODEL,), 0.02),
    }


def reference(x_prompt, x_sample, norm1_g, w_in, hy_conv_w, hy_conv_b, filt_w1, filt_b1,
              filt_w2, filt_b2, filt_w3, filt_b3, filt_freq, filt_w4, hy_bias, hy_out_g,
              ret_gn_g, ret_gn_b, w_out, norm2_g, w_up, ffn_conv_w, ffn_conv_b, w_down, final_g):
    prm = {
        'norm1_g': norm1_g, 'w_in': w_in, 'hy_conv_w': hy_conv_w, 'hy_conv_b': hy_conv_b,
        'filt_w1': filt_w1, 'filt_b1': filt_b1, 'filt_w2': filt_w2, 'filt_b2': filt_b2,
        'filt_w3': filt_w3, 'filt_b3': filt_b3, 'filt_freq': filt_freq, 'filt_w4': filt_w4,
        'hy_bias': hy_bias, 'hy_out_g': hy_out_g, 'ret_gn_g': ret_gn_g, 'ret_gn_b': ret_gn_b,
        'w_out': w_out, 'norm2_g': norm2_g, 'w_up': w_up, 'ffn_conv_w': ffn_conv_w,
        'ffn_conv_b': ffn_conv_b, 'w_down': w_down, 'final_g': final_g,
    }
    y_prompt = trunk(x_prompt, prm)
    y_sample = trunk(x_sample, prm)
    return (y_prompt, y_sample)
```

```python
import functools
import math

import jax
import jax.numpy as jnp
from jax import lax
from jax.experimental import pallas as pl
from jax.experimental.pallas import tpu as pltpu

F32 = jnp.float32
BF16 = jnp.bfloat16

D_MODEL = 1024
HY_W = D_MODEL // 2
RT_W = D_MODEL - HY_W
HY_ORDER = 2
HY_COLS = (HY_ORDER + 1) * HY_W
RT_COLS = 4 * RT_W
FILTER_BANDS = 16
FILTER_HIDDEN = 64
DECAY_TARGET = 1e-2
FAST_DECAY_PCT = 0.3
SLOW_DECAY_PCT = 1.5
RT_HEADS = 8
RT_HEAD_DIM = RT_W // RT_HEADS
ROPE_BASE = 10000.0
DECAY_OFFSET_FWD = 5.0
DECAY_OFFSET_BWD = 5.5
D_FF = ((8 * D_MODEL // 3 + 127) // 128) * 128
EPS = 1e-6

LANES = 128
HALO = 8
CONV_BLOCKS = 4
CONV_CT = 256
ROW_TILE = 512
FF_CHUNK = 256
RET_CHUNK = 128
RET_LANES = 256
CMAC_ROWS = 32
VMEM_LIMIT = 58 * 1024 * 1024


def _params(*sem):
    return pltpu.CompilerParams(dimension_semantics=sem, vmem_limit_bytes=VMEM_LIMIT)


def _dot(a, b):
    return jnp.dot(a, b, preferred_element_type=F32)


def _dot_hi(a, b):
    return jnp.dot(a, b, preferred_element_type=F32, precision=lax.Precision.HIGHEST)


def _const_spec(shape):
    nd = len(shape)
    return pl.BlockSpec(shape, lambda *_: (0,) * nd, pipeline_mode=pl.Buffered(1))


def _dft_tables(P):
    r = jnp.arange(2 * P, dtype=jnp.int32)[:, None]
    t = jnp.arange(P, dtype=jnp.int32)[None, :]
    m = ((2 * (r % P) + 1) * t) % (4 * P)
    ang = m.astype(F32) * (math.pi / (2 * P))
    fwd = jnp.where(r < P, jnp.cos(ang), -jnp.sin(ang))
    return fwd.astype(BF16), fwd.T.astype(BF16)


def _rotary_tables(L):
    half = RT_HEAD_DIM // 2
    inv = ROPE_BASE ** (-jnp.arange(0, RT_HEAD_DIM, 2, dtype=F32) / RT_HEAD_DIM)
    ang = jnp.arange(L, dtype=F32)[:, None] * inv[None, :]
    cos, sin = jnp.cos(ang), jnp.sin(ang)
    reps = LANES // RT_HEAD_DIM
    cos_t = jnp.tile(jnp.concatenate([cos, cos], axis=1), (1, reps))
    sin_t = jnp.tile(jnp.concatenate([-sin, sin], axis=1), (1, reps))
    return cos_t, sin_t


def _retention_tables():
    C = RET_CHUNK
    heads = jnp.arange(RT_HEADS, dtype=F32)
    log_f = jnp.log(1.0 - 2.0 ** (-DECAY_OFFSET_FWD - heads))
    log_b = jnp.log(1.0 - 2.0 ** (-DECAY_OFFSET_BWD - heads))
    idx = jnp.arange(C, dtype=F32)
    diff = idx[:, None] - idx[None, :]
    lower = diff >= 0
    d_f = jnp.where(lower[None], jnp.exp(jnp.where(lower, diff, 0.0)[None] * log_f[:, None, None]), 0.0)
    d_b = jnp.where(~lower[None], jnp.exp(jnp.where(~lower, -diff, 0.0)[None] * log_b[:, None, None]), 0.0)
    dmask = (d_f + d_b).reshape(RT_HEADS // 2, 2 * C, C)

    def lanes(tab):
        rows = tab.shape[1]
        t = jnp.repeat(tab[:, :, None], RT_HEAD_DIM, axis=2)
        t = t.reshape(RT_HEADS // 2, 2, rows, RT_HEAD_DIM).transpose(0, 2, 1, 3)
        return t.reshape(RT_HEADS // 2, rows, LANES)

    qd_f = lanes(jnp.exp((idx + 1.0)[None, :] * log_f[:, None]))
    kd_f = lanes(jnp.exp((C - 1.0 - idx)[None, :] * log_f[:, None]))
    qd_b = lanes(jnp.exp((C - idx)[None, :] * log_b[:, None]))
    kd_b = lanes(jnp.exp(idx[None, :] * log_b[:, None]))
    cd_f = lanes(jnp.exp(C * log_f)[:, None])
    cd_b = lanes(jnp.exp(C * log_b)[:, None])
    return dmask, qd_f, kd_f, qd_b, kd_b, cd_f, cd_b


def _filter_kernel(w1_ref, b1_ref, w2_ref, b2_ref, w3_ref, b3_ref, fr_ref, w4_ref, band_ref,
                   delta_ref, fwd_ref, g_ref, *, P, L, n):
    d = pl.program_id(0) - (n - 1)
    row = lax.broadcasted_iota(jnp.int32, (P, 1), 0)
    lane = lax.broadcasted_iota(jnp.int32, (1, LANES), 1)
    freq = fr_ref[...]

    def hidden(e):
        pos = jnp.abs(e * P + row).astype(F32)
        t = pos * (1.0 / (L - 1))
        ang = (2.0 * math.pi / L) * pos * band_ref[...]
        z = jnp.where(lane == 0, t,
                      jnp.where(lane <= FILTER_BANDS, jnp.cos(ang),
                                jnp.where(lane <= 2 * FILTER_BANDS, -jnp.sin(ang), 0.0)))
        h = jnp.sin(freq * (_dot_hi(z, w1_ref[...]) + b1_ref[...]))
        h = jnp.sin(freq * (_dot_hi(h, w2_ref[...]) + b2_ref[...]))
        h = jnp.sin(freq * (_dot_hi(h, w3_ref[...]) + b3_ref[...]))
        return h, t

    h1, t1 = hidden(d)
    h2, t2 = hidden(d - 1)
    dir1 = (d < 0).astype(jnp.int32)
    dir2 = (d < 1).astype(jnp.int32)
    sgn = (1 - 2 * (row & 1)).astype(F32)
    fwd = fwd_ref[...]
    for c in range(g_ref.shape[2] // CONV_CT):
        cs = pl.ds(c * CONV_CT, CONV_CT)
        dl = delta_ref[:, cs]
        f1 = _dot_hi(h1, w4_ref[pl.ds(dir1, 1), :, cs][0]) * jnp.exp(-t1 * dl)
        f2 = _dot_hi(h2, w4_ref[pl.ds(dir2, 1), :, cs][0]) * jnp.exp(-t2 * dl)
        f2 = jnp.where(row == 0, 0.0, f2)
        s1 = _dot(fwd, f1.astype(BF16))
        s2 = _dot(fwd, f2.astype(BF16))
        g_ref[0, pl.ds(0, P), cs] = s1[:P] - sgn * s2[P:]
        g_ref[0, pl.ds(P, P), cs] = s1[P:] + sgn * s2[:P]


def _filter_spectra(L, n, fwd, w1, b1, w2, b2, w3, b3, freq, w4):
    P = L // n
    cols = HY_ORDER * HY_W
    w1p = jnp.zeros((LANES, FILTER_HIDDEN), F32).at[: w1.shape[0]].set(w1)
    bands = jnp.linspace(1e-4, FILTER_BANDS - 1, FILTER_BANDS, dtype=F32)
    band_ext = jnp.zeros((1, LANES), F32).at[0, 1:1 + FILTER_BANDS].set(bands)
    band_ext = band_ext.at[0, 1 + FILTER_BANDS:1 + 2 * FILTER_BANDS].set(bands)
    deltas = jnp.abs(jnp.linspace(math.log(DECAY_TARGET) / FAST_DECAY_PCT,
                                  math.log(DECAY_TARGET) / SLOW_DECAY_PCT, HY_W, dtype=F32))
    delta = jnp.tile(deltas, HY_ORDER)[None, :]
    w4r = w4.reshape(FILTER_HIDDEN, 2, cols).transpose(1, 0, 2)
    row2 = lambda a: a.reshape(1, -1)
    args = (w1p, row2(b1), w2, row2(b2), w3, row2(b3), row2(freq), w4r, band_ext, delta, fwd)
    return pl.pallas_call(
        functools.partial(_filter_kernel, P=P, L=L, n=n),
        grid=(2 * n - 1,),
        in_specs=[_const_spec(a.shape) for a in args],
        out_specs=pl.BlockSpec((1, 2 * P, cols), lambda d: (d, 0, 0)),
        out_shape=jax.ShapeDtypeStruct((2 * n - 1, 2 * P, cols), F32),
        compiler_params=_params("arbitrary"),
        name="hyena_filter",
    )(*args)


def _in_kernel(xp_ref, xm_ref, xn_ref, g1_ref, wh_ref, wr_ref, cw_ref, cb_ref, cos_ref, sin_ref,
               hy_ref, rt_ref, pe_scr, *, tm):
    i = pl.program_id(1)
    last = pl.num_programs(1) - 1
    xe = jnp.concatenate([xp_ref[0], xm_ref[0], xn_ref[0]], axis=0)
    xe = xe * lax.rsqrt(jnp.mean(xe * xe, axis=-1, keepdims=True) + EPS) * g1_ref[...]
    row = lax.broadcasted_iota(jnp.int32, (tm + 2 * HALO, 1), 0)
    outside = ((row < HALO) & (i == 0)) | ((row >= tm + HALO) & (i == last))
    xe = jnp.where(outside, 0.0, xe)
    pe_scr[...] = _dot(xe.astype(BF16), wh_ref[...])
    hy = (pe_scr[pl.ds(HALO - 1, tm), :] * cw_ref[0:1, :] + pe_scr[pl.ds(HALO, tm), :] * cw_ref[1:2, :]
          + pe_scr[pl.ds(HALO + 1, tm), :] * cw_ref[2:3, :] + cb_ref[...])
    hy_ref[0] = hy.astype(BF16)

    pr = _dot(xe[HALO:HALO + tm].astype(BF16), wr_ref[...])
    cos = cos_ref[...]
    sin = sin_ref[...]
    lane = lax.broadcasted_iota(jnp.int32, (1, LANES), 1)
    first_half = (lane & (RT_HEAD_DIM // 2)) == 0
    for j in range(2 * RT_W // LANES):
        xg = pr[:, j * LANES:(j + 1) * LANES]
        partner = jnp.where(first_half, pltpu.roll(xg, LANES - RT_HEAD_DIM // 2, 1),
                            pltpu.roll(xg, RT_HEAD_DIM // 2, 1))
        r = xg * cos + partner * sin
        if j >= RT_W // LANES:
            r = r * RT_HEAD_DIM ** -0.5
        rt_ref[0, :, j * LANES:(j + 1) * LANES] = r.astype(BF16)
    rt_ref[0, :, 2 * RT_W:] = pr[:, 2 * RT_W:].astype(BF16)


def _halo_specs(tm, L, D):
    blocks = tm // HALO
    prev = pl.BlockSpec((1, HALO, D), lambda b, i: (b, jnp.maximum(i * blocks - 1, 0), 0))
    main = pl.BlockSpec((1, tm, D), lambda b, i: (b, i, 0))
    nxt = pl.BlockSpec((1, HALO, D), lambda b, i: (b, jnp.minimum((i + 1) * blocks, L // HALO - 1), 0))
    return [prev, main, nxt]


def _in_proj(x, g1, w_in, cw, cb, cos_t, sin_t):
    B, L, D = x.shape
    tm = min(ROW_TILE, L)
    wh = w_in[:, :HY_COLS].astype(BF16)
    wr = w_in[:, HY_COLS:].astype(BF16)
    consts = (g1.reshape(1, D), wh, wr, cw, cb.reshape(1, -1))
    return pl.pallas_call(
        functools.partial(_in_kernel, tm=tm),
        grid=(B, L // tm),
        in_specs=_halo_specs(tm, L, D) + [_const_spec(a.shape) for a in consts]
        + [pl.BlockSpec((tm, LANES), lambda b, i: (i, 0))] * 2,
        out_specs=[pl.BlockSpec((1, tm, HY_COLS), lambda b, i: (b, i, 0)),
                   pl.BlockSpec((1, tm, RT_COLS), lambda b, i: (b, i, 0))],
        out_shape=[jax.ShapeDtypeStruct((B, L, HY_COLS), BF16),
                   jax.ShapeDtypeStruct((B, L, RT_COLS), BF16)],
        scratch_shapes=[pltpu.VMEM((tm + 2 * HALO, HY_COLS), F32)],
        compiler_params=_params("parallel", "arbitrary"),
        name="in_proj",
    )(x, x, x, *consts, cos_t, sin_t)


def _conv_kernel(u_ref, x_ref, g_ref, bias_ref, fwd_ref, inv_ref, o_ref, spec_scr, prod_scr, *, P, n):
    for j in range(n):
        spec_scr[j] = _dot(fwd_ref[...], u_ref[0, pl.ds(j * P, P), :])
    for i in range(n):
        def rows(t, carry, i=i):
            r_re = pl.ds(pl.multiple_of(t * CMAC_ROWS, CMAC_ROWS), CMAC_ROWS)
            r_im = pl.ds(pl.multiple_of(P + t * CMAC_ROWS, CMAC_ROWS), CMAC_ROWS)
            acc_re = jnp.zeros((CMAC_ROWS, CONV_CT), F32)
            acc_im = jnp.zeros((CMAC_ROWS, CONV_CT), F32)
            for j in range(n):
                d = i - j + n - 1
                g_re, g_im = g_ref[d, r_re, :], g_ref[d, r_im, :]
                u_re, u_im = spec_scr[j, r_re, :], spec_scr[j, r_im, :]
                acc_re += g_re * u_re - g_im * u_im
                acc_im += g_re * u_im + g_im * u_re
            prod_scr[r_re, :] = acc_re.astype(BF16)
            prod_scr[r_im, :] = acc_im.astype(BF16)
            return carry
        lax.fori_loop(0, P // CMAC_ROWS, rows, 0)
        y = _dot(inv_ref[...], prod_scr[...]) * (1.0 / P)
        blk = pl.ds(i * P, P)
        u = u_ref[0, blk, :].astype(F32)
        o_ref[0, blk, :] = (x_ref[0, blk, :].astype(F32) * (y + u * bias_ref[...])).astype(BF16)


def _long_conv(u, u_col, gate, gate_col, spectra, order, bias, fwd, inv, n):
    B, L = u.shape[:2]
    P = L // n
    nct = HY_W // CONV_CT
    return pl.pallas_call(
        functools.partial(_conv_kernel, P=P, n=n),
        grid=(nct, B),
        in_specs=[pl.BlockSpec((1, L, CONV_CT), lambda c, b: (b, 0, u_col * nct + c)),
                  pl.BlockSpec((1, L, CONV_CT), lambda c, b: (b, 0, gate_col * nct + c)),
                  pl.BlockSpec((2 * n - 1, 2 * P, CONV_CT), lambda c, b: (0, 0, order * nct + c),
                               pipeline_mode=pl.Buffered(1)),
                  pl.BlockSpec((1, CONV_CT), lambda c, b: (0, c)),
                  _const_spec(fwd.shape), _const_spec(inv.shape)],
        out_specs=pl.BlockSpec((1, L, CONV_CT), lambda c, b: (b, 0, c)),
        out_shape=jax.ShapeDtypeStruct((B, L, HY_W), BF16),
        scratch_shapes=[pltpu.VMEM((n, 2 * P, CONV_CT), F32), pltpu.VMEM((2 * P, CONV_CT), BF16)],
        compiler_params=_params("arbitrary", "arbitrary"),
        name=f"long_conv{order}",
    )(u, gate, spectra, bias.reshape(1, -1), fwd, inv)


def _ret_kernel(q_ref, k_ref, v_ref, g_ref, dm_ref, qdf_ref, kdf_ref, qdb_ref, kdb_ref, cdf_ref, cdb_ref,
                avg_ref, gng_ref, gnb_ref, o_ref, acc_scr, *, L):
    C = RET_CHUNK
    pairs = RET_LANES // LANES
    nchunk = L // C
    lane = lax.broadcasted_iota(jnp.int32, (1, LANES), 1)
    head_a = lane < RT_HEAD_DIM
    ri = lax.broadcasted_iota(jnp.int32, (LANES, LANES), 0)
    ci = lax.broadcasted_iota(jnp.int32, (LANES, LANES), 1)
    same_head = ((ri < RT_HEAD_DIM) == (ci < RT_HEAD_DIM)).astype(F32)
    zero = jnp.zeros((), BF16)

    def split_heads(a):
        return jnp.concatenate([jnp.where(head_a, a, zero), jnp.where(head_a, zero, a)], axis=0)

    def state_update(s, kc, vc, kd, cd):
        kt = (kc.astype(F32) * kd).T.astype(BF16)
        return s * cd + _dot(kt, vc) * same_head

    def fwd_body(nidx, states):
        rows = pl.ds(pl.multiple_of(nidx * C, C), C)
        new = []
        for p in range(pairs):
            ls = slice(p * LANES, (p + 1) * LANES)
            qc, kc, vc = q_ref[0, rows, ls], k_ref[0, rows, ls], v_ref[0, rows, ls]
            s = lax.dot_general(split_heads(qc), kc, (((1,), (1,)), ((), ())), preferred_element_type=F32)
            s = (s * dm_ref[p]).astype(BF16)
            o = _dot(jnp.concatenate([s[:C], s[C:]], axis=1), split_heads(vc))
            o += _dot((qc.astype(F32) * qdf_ref[p]).astype(BF16), states[p].astype(BF16))
            acc_scr[rows, ls] = o
            new.append(state_update(states[p], kc, vc, kdf_ref[p], cdf_ref[p]))
        return tuple(new)

    def bwd_body(it, states):
        rows = pl.ds(pl.multiple_of((nchunk - 1 - it) * C, C), C)
        new = []
        for p in range(pairs):
            ls = slice(p * LANES, (p + 1) * LANES)
            qc, kc, vc = q_ref[0, rows, ls], k_ref[0, rows, ls], v_ref[0, rows, ls]
            acc_scr[rows, ls] += _dot((qc.astype(F32) * qdb_ref[p]).astype(BF16), states[p].astype(BF16))
            new.append(state_update(states[p], kc, vc, kdb_ref[p], cdb_ref[p]))
        return tuple(new)

    init = tuple(jnp.zeros((LANES, LANES), F32) for _ in range(pairs))
    lax.fori_loop(0, nchunk, fwd_body, init)
    lax.fori_loop(0, nchunk, bwd_body, init)

    tn = min(ROW_TILE, L)

    def norm_body(it, carry):
        rows = pl.ds(pl.multiple_of(it * tn, tn), tn)
        o = acc_scr[rows, :]
        oc = o - _dot_hi(o, avg_ref[...])
        var = _dot_hi(oc * oc, avg_ref[...])
        on = oc * lax.rsqrt(var + EPS) * gng_ref[...] + gnb_ref[...]
        g = g_ref[0, rows, :].astype(F32)
        o_ref[0, rows, :] = (on * g * (1.0 / (1.0 + jnp.exp(-g)))).astype(BF16)
        return carry

    lax.fori_loop(0, L // tn, norm_body, 0)


def _retention(rt, tables, gn_g, gn_b):
    B, L, _ = rt.shape
    groups = RT_W // RET_LANES
    pairs = RET_LANES // LANES
    li = jnp.arange(RET_LANES) // RT_HEAD_DIM
    avg = (li[:, None] == li[None, :]).astype(F32) / RT_HEAD_DIM
    col = lambda which: pl.BlockSpec((1, L, RET_LANES), lambda g, b: (b, 0, which * groups + g))
    tab = lambda a: pl.BlockSpec((pairs,) + a.shape[1:], lambda g, b: (g, 0, 0))
    vec = pl.BlockSpec((1, RET_LANES), lambda g, b: (0, g))
    return pl.pallas_call(
        functools.partial(_ret_kernel, L=L),
        grid=(groups, B),
        in_specs=[col(0), col(1), col(2), col(3)] + [tab(a) for a in tables] + [_const_spec(avg.shape), vec, vec],
        out_specs=pl.BlockSpec((1, L, RET_LANES), lambda g, b: (b, 0, g)),
        out_shape=jax.ShapeDtypeStruct((B, L, RT_W), BF16),
        scratch_shapes=[pltpu.VMEM((L, RET_LANES), F32)],
        compiler_params=_params("arbitrary", "arbitrary"),
        name="retention",
    )(rt, rt, rt, rt, *tables, avg, gn_g.reshape(1, -1), gn_b.reshape(1, -1))


def _mix_kernel(x_ref, zh_ref, zr_ref, hg_ref, wh_ref, wr_ref, o_ref):
    z = zh_ref[0].astype(F32)
    yh = z * lax.rsqrt(jnp.mean(z * z, axis=-1, keepdims=True) + EPS) * hg_ref[...]
    o_ref[0] = x_ref[0] + _dot(yh.astype(BF16), wh_ref[...]) + _dot(zr_ref[0], wr_ref[...])


def _mix(x, zh, zr, hy_out_g, w_out):
    B, L, D = x.shape
    tm = min(ROW_TILE, L)
    wh = w_out[:HY_W].astype(BF16)
    wr = w_out[HY_W:].astype(BF16)
    tok = lambda w: pl.BlockSpec((1, tm, w), lambda b, i: (b, i, 0))
    consts = (hy_out_g.reshape(1, -1), wh, wr)
    return pl.pallas_call(
        _mix_kernel,
        grid=(B, L // tm),
        in_specs=[tok(D), tok(HY_W), tok(RT_W)] + [_const_spec(a.shape) for a in consts],
        out_specs=tok(D),
        out_shape=jax.ShapeDtypeStruct((B, L, D), F32),
        compiler_params=_params("parallel", "arbitrary"),
        name="out_proj",
    )(x, zh, zr, *consts)


def _gelu_tanh(a):
    return 0.5 * a * (1.0 + jnp.tanh(math.sqrt(2.0 / math.pi) * (a + 0.044715 * (a * a * a))))


def _ffn_kernel(xp_ref, xm_ref, xn_ref, g2_ref, wa_ref, wb_ref, cwa_ref, cwb_ref, cba_ref, cbb_ref,
                wd_ref, gf_ref, o_ref, ua_scr, ub_scr, *, tm):
    i = pl.program_id(1)
    last = pl.num_programs(1) - 1
    xe = jnp.concatenate([xp_ref[0], xm_ref[0], xn_ref[0]], axis=0)
    xe = xe * lax.rsqrt(jnp.mean(xe * xe, axis=-1, keepdims=True) + EPS) * g2_ref[...]
    row = lax.broadcasted_iota(jnp.int32, (tm + 2 * HALO, 1), 0)
    outside = ((row < HALO) & (i == 0)) | ((row >= tm + HALO) & (i == last))
    xe = jnp.where(outside, 0.0, xe).astype(BF16)

    def conv3(scr, cw_ref, cb_ref, cs):
        return (scr[pl.ds(HALO - 1, tm), :] * cw_ref[0:1, cs] + scr[pl.ds(HALO, tm), :] * cw_ref[1:2, cs]
                + scr[pl.ds(HALO + 1, tm), :] * cw_ref[2:3, cs] + cb_ref[:, cs])

    acc = jnp.zeros((tm, D_MODEL), F32)
    for f in range(D_FF // FF_CHUNK):
        cs = pl.ds(f * FF_CHUNK, FF_CHUNK)
        ua_scr[...] = _dot(xe, wa_ref[:, cs])
        ub_scr[...] = _dot(xe, wb_ref[:, cs])
        a = conv3(ua_scr, cwa_ref, cba_ref, cs)
        b = conv3(ub_scr, cwb_ref, cbb_ref, cs)
        acc += _dot((_gelu_tanh(a) * b).astype(BF16), wd_ref[cs, :])
    y = xm_ref[0] + acc
    o_ref[0] = y * lax.rsqrt(jnp.mean(y * y, axis=-1, keepdims=True) + EPS) * gf_ref[...]


def _ffn(x, norm2_g, w_up, conv_w, conv_b, w_down, final_g):
    B, L, D = x.shape
    tm = min(ROW_TILE, L)
    consts = (norm2_g.reshape(1, D), w_up[:, :D_FF].astype(BF16), w_up[:, D_FF:].astype(BF16),
              conv_w[:, :D_FF], conv_w[:, D_FF:], conv_b[:D_FF].reshape(1, -1), conv_b[D_FF:].reshape(1, -1),
              w_down.astype(BF16), final_g.reshape(1, D))
    return pl.pallas_call(
        functools.partial(_ffn_kernel, tm=tm),
        grid=(B, L // tm),
        in_specs=_halo_specs(tm, L, D) + [_const_spec(a.shape) for a in consts],
        out_specs=pl.BlockSpec((1, tm, D), lambda b, i: (b, i, 0)),
        out_shape=jax.ShapeDtypeStruct((B, L, D), F32),
        scratch_shapes=[pltpu.VMEM((tm + 2 * HALO, FF_CHUNK), F32)] * 2,
        compiler_params=_params("parallel", "arbitrary"),
        name="conv_ffn",
    )(x, x, x, *consts)


def _trunk(x, prm, spectra, fwd, inv, rot, ret_tables, n):
    hy, rt = _in_proj(x, prm["norm1_g"], prm["w_in"], prm["hy_conv_w"], prm["hy_conv_b"], *rot)
    z = _long_conv(hy, 0, hy, 1, spectra, 0, prm["hy_bias"][0], fwd, inv, n)
    z = _long_conv(z, 0, hy, 2, spectra, 1, prm["hy_bias"][1], fwd, inv, n)
    zr = _retention(rt, ret_tables, prm["ret_gn_g"], prm["ret_gn_b"])
    x = _mix(x, z, zr, prm["hy_out_g"], prm["w_out"])
    return _ffn(x, prm["norm2_g"], prm["w_up"], prm["ffn_conv_w"], prm["ffn_conv_b"], prm["w_down"],
                prm["final_g"])


def _layer(xs, prm, n=CONV_BLOCKS):
    L = xs[0].shape[1]
    assert all(x.shape[1] == L for x in xs) and L % (n * CMAC_ROWS) == 0 and L % RET_CHUNK == 0
    fwd, inv = _dft_tables(L // n)
    spectra = _filter_spectra(L, n, fwd, prm["filt_w1"], prm["filt_b1"], prm["filt_w2"], prm["filt_b2"],
                              prm["filt_w3"], prm["filt_b3"], prm["filt_freq"], prm["filt_w4"])
    rot = _rotary_tables(L)
    ret_tables = _retention_tables()
    return tuple(_trunk(x, prm, spectra, fwd, inv, rot, ret_tables, n) for x in xs)


def kernel(x_prompt, x_sample, norm1_g, w_in, hy_conv_w, hy_conv_b, filt_w1, filt_b1, filt_w2, filt_b2,
           filt_w3, filt_b3, filt_freq, filt_w4, hy_bias, hy_out_g, ret_gn_g, ret_gn_b, w_out, norm2_g,
           w_up, ffn_conv_w, ffn_conv_b, w_down, final_g):
    layer = dict(norm1_g=norm1_g, w_in=w_in, hy_conv_w=hy_conv_w, hy_conv_b=hy_conv_b, filt_w1=filt_w1,
                 filt_b1=filt_b1, filt_w2=filt_w2, filt_b2=filt_b2, filt_w3=filt_w3, filt_b3=filt_b3,
                 filt_freq=filt_freq, filt_w4=filt_w4, hy_bias=hy_bias, hy_out_g=hy_out_g,
                 ret_gn_g=ret_gn_g, ret_gn_b=ret_gn_b, w_out=w_out, norm2_g=norm2_g, w_up=w_up,
                 ffn_conv_w=ffn_conv_w, ffn_conv_b=ffn_conv_b, w_down=w_down)
    prm = {k: v[0] for k, v in layer.items()}
    prm["final_g"] = final_g
    return _layer((x_prompt, x_sample), prm)
```

```python
import functools
import math

import jax
import jax.numpy as jnp
from jax import lax
from jax.experimental import pallas as pl
from jax.experimental.pallas import tpu as pltpu

F32 = jnp.float32
BF16 = jnp.bfloat16

D_MODEL = 1024
HY_W = D_MODEL // 2
RT_W = D_MODEL - HY_W
HY_ORDER = 2
HY_COLS = (HY_ORDER + 1) * HY_W
RT_COLS = 4 * RT_W
FILTER_BANDS = 16
FILTER_HIDDEN = 64
DECAY_TARGET = 1e-2
FAST_DECAY_PCT = 0.3
SLOW_DECAY_PCT = 1.5
RT_HEADS = 8
RT_HEAD_DIM = RT_W // RT_HEADS
ROPE_BASE = 10000.0
DECAY_OFFSET_FWD = 5.0
DECAY_OFFSET_BWD = 5.5
D_FF = ((8 * D_MODEL // 3 + 127) // 128) * 128
EPS = 1e-6

LANES = 128
HALO = 8
CONV_BLOCKS = 4
CONV_CT = 256
ROW_TILE = 512
FF_CHUNK = 256
RET_CHUNK = 128
RET_UNROLL = 2
RET_LANES = 256
CMAC_ROWS = 32
VMEM_LIMIT = 58 * 1024 * 1024


def _params(*sem):
    return pltpu.CompilerParams(dimension_semantics=sem, vmem_limit_bytes=VMEM_LIMIT)


def _dot(a, b):
    return jnp.dot(a, b, preferred_element_type=F32)


def _dot_hi(a, b):
    return jnp.dot(a, b, preferred_element_type=F32, precision=lax.Precision.HIGHEST)


def _const_spec(shape):
    nd = len(shape)
    return pl.BlockSpec(shape, lambda *_: (0,) * nd, pipeline_mode=pl.Buffered(1))


def _dft_tables(P):
    r = jnp.arange(2 * P, dtype=jnp.int32)[:, None]
    t = jnp.arange(P, dtype=jnp.int32)[None, :]
    m = ((2 * (r % P) + 1) * t) % (4 * P)
    ang = m.astype(F32) * (math.pi / (2 * P))
    fwd = jnp.where(r < P, jnp.cos(ang), -jnp.sin(ang))
    return fwd.astype(BF16), fwd.T.astype(BF16)


def _rotary_tables(L):
    inv = ROPE_BASE ** (-jnp.arange(0, RT_HEAD_DIM, 2, dtype=F32) / RT_HEAD_DIM)
    ang = jnp.arange(L, dtype=F32)[:, None] * inv[None, :]
    cos, sin = jnp.cos(ang), jnp.sin(ang)
    reps = LANES // RT_HEAD_DIM
    cos_t = jnp.tile(jnp.concatenate([cos, cos], axis=1), (1, reps))
    sin_t = jnp.tile(jnp.concatenate([-sin, sin], axis=1), (1, reps))
    return cos_t, sin_t


def _retention_tables():
    C = RET_CHUNK
    heads = jnp.arange(RT_HEADS, dtype=F32)
    log_f = jnp.log(1.0 - 2.0 ** (-DECAY_OFFSET_FWD - heads))
    log_b = jnp.log(1.0 - 2.0 ** (-DECAY_OFFSET_BWD - heads))
    idx = jnp.arange(C, dtype=F32)
    diff = idx[:, None] - idx[None, :]
    lower = diff >= 0
    d_f = jnp.where(lower[None], jnp.exp(jnp.where(lower, diff, 0.0)[None] * log_f[:, None, None]), 0.0)
    d_b = jnp.where(~lower[None], jnp.exp(jnp.where(~lower, -diff, 0.0)[None] * log_b[:, None, None]), 0.0)
    dmask = (d_f + d_b).reshape(RT_HEADS // 2, 2 * C, C)

    def lanes(tab):
        rows = tab.shape[1]
        t = jnp.repeat(tab[:, :, None], RT_HEAD_DIM, axis=2)
        t = t.reshape(RT_HEADS // 2, 2, rows, RT_HEAD_DIM).transpose(0, 2, 1, 3)
        return t.reshape(RT_HEADS // 2, rows, LANES)

    qd_f = lanes(jnp.exp((idx + 1.0)[None, :] * log_f[:, None]))
    kd_f = lanes(jnp.exp((C - 1.0 - idx)[None, :] * log_f[:, None]))
    qd_b = lanes(jnp.exp((C - idx)[None, :] * log_b[:, None]))
    kd_b = lanes(jnp.exp(idx[None, :] * log_b[:, None]))
    cd_f = lanes(jnp.exp(C * log_f)[:, None])
    cd_b = lanes(jnp.exp(C * log_b)[:, None])
    return dmask, qd_f, kd_f, qd_b, kd_b, cd_f, cd_b


def _filter_kernel(w1_ref, b1_ref, w2_ref, b2_ref, w3_ref, b3_ref, fr_ref, w4_ref, band_ref,
                   delta_ref, fwd_ref, g_ref, prev_scr, tap_scr, *, P, L, n):
    step = pl.program_id(0)
    e = step - n
    row = lax.broadcasted_iota(jnp.int32, (P, 1), 0)
    lane = lax.broadcasted_iota(jnp.int32, (1, LANES), 1)
    freq = fr_ref[...]
    pos = jnp.abs(e * P + row).astype(F32)
    t = pos * (1.0 / (L - 1))
    ang = (2.0 * math.pi / L) * pos * band_ref[...]
    z = jnp.where(lane == 0, t,
                  jnp.where(lane <= FILTER_BANDS, jnp.cos(ang),
                            jnp.where(lane <= 2 * FILTER_BANDS, -jnp.sin(ang), 0.0)))
    h = jnp.sin(freq * (_dot_hi(z, w1_ref[...]) + b1_ref[...]))
    h = jnp.sin(freq * (_dot_hi(h, w2_ref[...]) + b2_ref[...]))
    h = jnp.sin(freq * (_dot_hi(h, w3_ref[...]) + b3_ref[...]))
    direction = (e < 0).astype(jnp.int32)
    sgn = (1 - 2 * (row & 1)).astype(F32)
    for c in range(g_ref.shape[2] // CONV_CT):
        cs = pl.ds(c * CONV_CT, CONV_CT)
        taps = _dot_hi(h, w4_ref[pl.ds(direction, 1), :, cs][0]) * jnp.exp(-t * delta_ref[:, cs])
        taps = taps.astype(BF16)
        s1 = _dot(fwd_ref[...], taps)

        @pl.when(step > 0)
        def _():
            s2_re = prev_scr[pl.ds(0, P), cs] - tap_scr[:, cs]
            s2_im = prev_scr[pl.ds(P, P), cs]
            g_ref[0, pl.ds(0, P), cs] = s1[:P] - sgn * s2_im
            g_ref[0, pl.ds(P, P), cs] = s1[P:] + sgn * s2_re

        prev_scr[:, cs] = s1
        tap_scr[:, cs] = taps[0:1].astype(F32)


def _filter_spectra(L, n, fwd, w1, b1, w2, b2, w3, b3, freq, w4):
    P = L // n
    cols = HY_ORDER * HY_W
    w1p = jnp.zeros((LANES, FILTER_HIDDEN), F32).at[: w1.shape[0]].set(w1)
    bands = jnp.linspace(1e-4, FILTER_BANDS - 1, FILTER_BANDS, dtype=F32)
    band_ext = jnp.zeros((1, LANES), F32).at[0, 1:1 + FILTER_BANDS].set(bands)
    band_ext = band_ext.at[0, 1 + FILTER_BANDS:1 + 2 * FILTER_BANDS].set(bands)
    deltas = jnp.abs(jnp.linspace(math.log(DECAY_TARGET) / FAST_DECAY_PCT,
                                  math.log(DECAY_TARGET) / SLOW_DECAY_PCT, HY_W, dtype=F32))
    delta = jnp.tile(deltas, HY_ORDER)[None, :]
    w4r = w4.reshape(FILTER_HIDDEN, 2, cols).transpose(1, 0, 2)
    row2 = lambda a: a.reshape(1, -1)
    args = (w1p, row2(b1), w2, row2(b2), w3, row2(b3), row2(freq), w4r, band_ext, delta, fwd)
    return pl.pallas_call(
        functools.partial(_filter_kernel, P=P, L=L, n=n),
        grid=(2 * n,),
        in_specs=[_const_spec(a.shape) for a in args],
        out_specs=pl.BlockSpec((1, 2 * P, cols), lambda s: (jnp.maximum(s - 1, 0), 0, 0)),
        out_shape=jax.ShapeDtypeStruct((2 * n - 1, 2 * P, cols), F32),
        scratch_shapes=[pltpu.VMEM((2 * P, cols), F32), pltpu.VMEM((1, cols), F32)],
        compiler_params=_params("arbitrary"),
        name="hyena_filter",
    )(*args)


def _in_kernel(xp_ref, xm_ref, xn_ref, g1_ref, wh_ref, wr_ref, cw_ref, cb_ref, cos_ref, sin_ref,
               hy_ref, rt_ref, pe_scr, *, tm):
    i = pl.program_id(1)
    last = pl.num_programs(1) - 1
    xe = jnp.concatenate([xp_ref[0], xm_ref[0], xn_ref[0]], axis=0)
    xe = xe * lax.rsqrt(jnp.mean(xe * xe, axis=-1, keepdims=True) + EPS) * g1_ref[...]
    row = lax.broadcasted_iota(jnp.int32, (tm + 2 * HALO, 1), 0)
    outside = ((row < HALO) & (i == 0)) | ((row >= tm + HALO) & (i == last))
    xe = jnp.where(outside, 0.0, xe)
    pe_scr[...] = _dot(xe.astype(BF16), wh_ref[...])
    hy = (pe_scr[pl.ds(HALO - 1, tm), :] * cw_ref[0:1, :] + pe_scr[pl.ds(HALO, tm), :] * cw_ref[1:2, :]
          + pe_scr[pl.ds(HALO + 1, tm), :] * cw_ref[2:3, :] + cb_ref[...])
    hy_ref[0] = hy.astype(BF16)

    pr = _dot(xe[HALO:HALO + tm].astype(BF16), wr_ref[...])
    cos = cos_ref[...]
    sin = sin_ref[...]
    lane = lax.broadcasted_iota(jnp.int32, (1, LANES), 1)
    first_half = (lane & (RT_HEAD_DIM // 2)) == 0
    for j in range(2 * RT_W // LANES):
        xg = pr[:, j * LANES:(j + 1) * LANES]
        partner = jnp.where(first_half, pltpu.roll(xg, LANES - RT_HEAD_DIM // 2, 1),
                            pltpu.roll(xg, RT_HEAD_DIM // 2, 1))
        r = xg * cos + partner * sin
        if j >= RT_W // LANES:
            r = r * RT_HEAD_DIM ** -0.5
        rt_ref[0, :, j * LANES:(j + 1) * LANES] = r.astype(BF16)
    rt_ref[0, :, 2 * RT_W:] = pr[:, 2 * RT_W:].astype(BF16)


def _halo_specs(tm, L, D):
    blocks = tm // HALO
    prev = pl.BlockSpec((1, HALO, D), lambda b, i: (b, jnp.maximum(i * blocks - 1, 0), 0))
    main = pl.BlockSpec((1, tm, D), lambda b, i: (b, i, 0))
    nxt = pl.BlockSpec((1, HALO, D), lambda b, i: (b, jnp.minimum((i + 1) * blocks, L // HALO - 1), 0))
    return [prev, main, nxt]


def _in_proj(x, g1, w_in, cw, cb, cos_t, sin_t):
    B, L, D = x.shape
    tm = min(ROW_TILE, L)
    wh = w_in[:, :HY_COLS].astype(BF16)
    wr = w_in[:, HY_COLS:].astype(BF16)
    consts = (g1.reshape(1, D), wh, wr, cw, cb.reshape(1, -1))
    return pl.pallas_call(
        functools.partial(_in_kernel, tm=tm),
        grid=(B, L // tm),
        in_specs=_halo_specs(tm, L, D) + [_const_spec(a.shape) for a in consts]
        + [pl.BlockSpec((tm, LANES), lambda b, i: (i, 0))] * 2,
        out_specs=[pl.BlockSpec((1, tm, HY_COLS), lambda b, i: (b, i, 0)),
                   pl.BlockSpec((1, tm, RT_COLS), lambda b, i: (b, i, 0))],
        out_shape=[jax.ShapeDtypeStruct((B, L, HY_COLS), BF16),
                   jax.ShapeDtypeStruct((B, L, RT_COLS), BF16)],
        scratch_shapes=[pltpu.VMEM((tm + 2 * HALO, HY_COLS), F32)],
        compiler_params=_params("parallel", "arbitrary"),
        name="in_proj",
    )(x, x, x, *consts, cos_t, sin_t)


def _conv_kernel(u_ref, x_ref, g_ref, bias_ref, fwd_ref, inv_ref, o_ref, spec_scr, prod_scr, *, P, n):
    for j in range(n):
        spec_scr[j] = _dot(fwd_ref[...], u_ref[0, pl.ds(j * P, P), :])
    for i in range(n):
        def rows(t, carry, i=i):
            r_re = pl.ds(pl.multiple_of(t * CMAC_ROWS, CMAC_ROWS), CMAC_ROWS)
            r_im = pl.ds(pl.multiple_of(P + t * CMAC_ROWS, CMAC_ROWS), CMAC_ROWS)
            acc_re = jnp.zeros((CMAC_ROWS, CONV_CT), F32)
            acc_im = jnp.zeros((CMAC_ROWS, CONV_CT), F32)
            for j in range(n):
                d = i - j + n - 1
                g_re, g_im = g_ref[d, r_re, :], g_ref[d, r_im, :]
                u_re, u_im = spec_scr[j, r_re, :], spec_scr[j, r_im, :]
                acc_re += g_re * u_re - g_im * u_im
                acc_im += g_re * u_im + g_im * u_re
            prod_scr[i, r_re, :] = acc_re.astype(BF16)
            prod_scr[i, r_im, :] = acc_im.astype(BF16)
            return carry
        lax.fori_loop(0, P // CMAC_ROWS, rows, 0)
    for i in range(n):
        y = _dot(inv_ref[...], prod_scr[i]) * (1.0 / P)
        blk = pl.ds(i * P, P)
        u = u_ref[0, blk, :].astype(F32)
        o_ref[0, blk, :] = (x_ref[0, blk, :].astype(F32) * (y + u * bias_ref[...])).astype(BF16)


def _long_conv(u, u_col, gate, gate_col, spectra, order, bias, fwd, inv, n):
    B, L = u.shape[:2]
    P = L // n
    nct = HY_W // CONV_CT
    return pl.pallas_call(
        functools.partial(_conv_kernel, P=P, n=n),
        grid=(nct, B),
        in_specs=[pl.BlockSpec((1, L, CONV_CT), lambda c, b: (b, 0, u_col * nct + c)),
                  pl.BlockSpec((1, L, CONV_CT), lambda c, b: (b, 0, gate_col * nct + c)),
                  pl.BlockSpec((2 * n - 1, 2 * P, CONV_CT), lambda c, b: (0, 0, order * nct + c),
                               pipeline_mode=pl.Buffered(1)),
                  pl.BlockSpec((1, CONV_CT), lambda c, b: (0, c)),
                  _const_spec(fwd.shape), _const_spec(inv.shape)],
        out_specs=pl.BlockSpec((1, L, CONV_CT), lambda c, b: (b, 0, c)),
        out_shape=jax.ShapeDtypeStruct((B, L, HY_W), BF16),
        scratch_shapes=[pltpu.VMEM((n, 2 * P, CONV_CT), F32), pltpu.VMEM((n, 2 * P, CONV_CT), BF16)],
        compiler_params=_params("arbitrary", "arbitrary"),
        name=f"long_conv{order}",
    )(u, gate, spectra, bias.reshape(1, -1), fwd, inv)


def _ret_kernel(q_ref, k_ref, v_ref, g_ref, dm_ref, qdf_ref, kdf_ref, qdb_ref, kdb_ref, cdf_ref, cdb_ref,
                avg_ref, gng_ref, gnb_ref, o_ref, acc_scr, *, L):
    C = RET_CHUNK
    pairs = RET_LANES // LANES
    nchunk = L // C
    lane = lax.broadcasted_iota(jnp.int32, (1, LANES), 1)
    head_a = lane < RT_HEAD_DIM
    ri = lax.broadcasted_iota(jnp.int32, (LANES, LANES), 0)
    ci = lax.broadcasted_iota(jnp.int32, (LANES, LANES), 1)
    same_head = ((ri < RT_HEAD_DIM) == (ci < RT_HEAD_DIM)).astype(F32)
    zero = jnp.zeros((), BF16)

    def split_heads(a):
        return jnp.concatenate([jnp.where(head_a, a, zero), jnp.where(head_a, zero, a)], axis=0)

    def state_update(s, kc, vc, kd, cd):
        kt = (kc.astype(F32) * kd).T.astype(BF16)
        return s * cd + _dot(kt, vc) * same_head

    def fwd_body(nidx, states):
        rows = pl.ds(pl.multiple_of(nidx * C, C), C)
        new = []
        for p in range(pairs):
            ls = slice(p * LANES, (p + 1) * LANES)
            qc, kc, vc = q_ref[0, rows, ls], k_ref[0, rows, ls], v_ref[0, rows, ls]
            s = lax.dot_general(split_heads(qc), kc, (((1,), (1,)), ((), ())), preferred_element_type=F32)
            s = (s * dm_ref[p]).astype(BF16)
            qd = (qc.astype(F32) * qdf_ref[p]).astype(BF16)
            lhs = jnp.concatenate([s[:C], s[C:], qd], axis=1)
            rhs = jnp.concatenate([split_heads(vc), states[p].astype(BF16)], axis=0)
            acc_scr[rows, ls] = _dot(lhs, rhs)
            new.append(state_update(states[p], kc, vc, kdf_ref[p], cdf_ref[p]))
        return tuple(new)

    def bwd_body(it, states):
        rows = pl.ds(pl.multiple_of((nchunk - 1 - it) * C, C), C)
        new = []
        for p in range(pairs):
            ls = slice(p * LANES, (p + 1) * LANES)
            qc, kc, vc = q_ref[0, rows, ls], k_ref[0, rows, ls], v_ref[0, rows, ls]
            acc_scr[rows, ls] += _dot((qc.astype(F32) * qdb_ref[p]).astype(BF16), states[p].astype(BF16))
            new.append(state_update(states[p], kc, vc, kdb_ref[p], cdb_ref[p]))
        return tuple(new)

    init = tuple(jnp.zeros((LANES, LANES), F32) for _ in range(pairs))
    lax.fori_loop(0, nchunk, fwd_body, init, unroll=RET_UNROLL)
    lax.fori_loop(0, nchunk, bwd_body, init, unroll=RET_UNROLL)

    tn = min(ROW_TILE, L)
    avg = avg_ref[...]

    def group_mean(a):
        hi = a.astype(BF16)
        lo = (a - hi.astype(F32)).astype(BF16)
        return _dot(hi, avg) + _dot(lo, avg)

    def norm_body(it, carry):
        rows = pl.ds(pl.multiple_of(it * tn, tn), tn)
        o = acc_scr[rows, :]
        oc = o - group_mean(o)
        var = group_mean(oc * oc)
        on = oc * lax.rsqrt(var + EPS) * gng_ref[...] + gnb_ref[...]
        g = g_ref[0, rows, :].astype(F32)
        o_ref[0, rows, :] = (on * g * (1.0 / (1.0 + jnp.exp(-g)))).astype(BF16)
        return carry

    lax.fori_loop(0, L // tn, norm_body, 0)


def _retention(rt, tables, gn_g, gn_b):
    B, L, _ = rt.shape
    groups = RT_W // RET_LANES
    pairs = RET_LANES // LANES
    li = jnp.arange(RET_LANES) // RT_HEAD_DIM
    avg = ((li[:, None] == li[None, :]).astype(F32) / RT_HEAD_DIM).astype(BF16)
    col = lambda which: pl.BlockSpec((1, L, RET_LANES), lambda g, b: (b, 0, which * groups + g))
    tab = lambda a: pl.BlockSpec((pairs,) + a.shape[1:], lambda g, b: (g, 0, 0))
    vec = pl.BlockSpec((1, RET_LANES), lambda g, b: (0, g))
    return pl.pallas_call(
        functools.partial(_ret_kernel, L=L),
        grid=(groups, B),
        in_specs=[col(0), col(1), col(2), col(3)] + [tab(a) for a in tables] + [_const_spec(avg.shape), vec, vec],
        out_specs=pl.BlockSpec((1, L, RET_LANES), lambda g, b: (b, 0, g)),
        out_shape=jax.ShapeDtypeStruct((B, L, RT_W), BF16),
        scratch_shapes=[pltpu.VMEM((L, RET_LANES), F32)],
        compiler_params=_params("arbitrary", "arbitrary"),
        name="retention",
    )(rt, rt, rt, rt, *tables, avg, gn_g.reshape(1, -1), gn_b.reshape(1, -1))


def _mix_kernel(x_ref, zh_ref, zr_ref, hg_ref, wh_ref, wr_ref, o_ref):
    z = zh_ref[0].astype(F32)
    yh = z * lax.rsqrt(jnp.mean(z * z, axis=-1, keepdims=True) + EPS) * hg_ref[...]
    o_ref[0] = x_ref[0] + _dot(yh.astype(BF16), wh_ref[...]) + _dot(zr_ref[0], wr_ref[...])


def _mix(x, zh, zr, hy_out_g, w_out):
    B, L, D = x.shape
    tm = min(ROW_TILE, L)
    wh = w_out[:HY_W].astype(BF16)
    wr = w_out[HY_W:].astype(BF16)
    tok = lambda w: pl.BlockSpec((1, tm, w), lambda b, i: (b, i, 0))
    consts = (hy_out_g.reshape(1, -1), wh, wr)
    return pl.pallas_call(
        _mix_kernel,
        grid=(B, L // tm),
        in_specs=[tok(D), tok(HY_W), tok(RT_W)] + [_const_spec(a.shape) for a in consts],
        out_specs=tok(D),
        out_shape=jax.ShapeDtypeStruct((B, L, D), F32),
        compiler_params=_params("parallel", "arbitrary"),
        name="out_proj",
    )(x, zh, zr, *consts)


def _gelu_tanh(a):
    return 0.5 * a * (1.0 + jnp.tanh(math.sqrt(2.0 / math.pi) * (a + 0.044715 * (a * a * a))))


def _ffn_kernel(xp_ref, xm_ref, xn_ref, g2_ref, wa_ref, wb_ref, cwa_ref, cwb_ref, cba_ref, cbb_ref,
                wd_ref, gf_ref, o_ref, ua_scr, ub_scr, *, tm):
    i = pl.program_id(1)
    last = pl.num_programs(1) - 1
    xe = jnp.concatenate([xp_ref[0], xm_ref[0], xn_ref[0]], axis=0)
    xe = xe * lax.rsqrt(jnp.mean(xe * xe, axis=-1, keepdims=True) + EPS) * g2_ref[...]
    row = lax.broadcasted_iota(jnp.int32, (tm + 2 * HALO, 1), 0)
    outside = ((row < HALO) & (i == 0)) | ((row >= tm + HALO) & (i == last))
    xe = jnp.where(outside, 0.0, xe).astype(BF16)

    def conv3(scr, cw_ref, cb_ref, cs):
        return (scr[pl.ds(HALO - 1, tm), :] * cw_ref[0:1, cs] + scr[pl.ds(HALO, tm), :] * cw_ref[1:2, cs]
                + scr[pl.ds(HALO + 1, tm), :] * cw_ref[2:3, cs] + cb_ref[:, cs])

    nf = D_FF // FF_CHUNK

    def up(f):
        cs = pl.ds(f * FF_CHUNK, FF_CHUNK)
        ua_scr[f % 2] = _dot(xe, wa_ref[:, cs])
        ub_scr[f % 2] = _dot(xe, wb_ref[:, cs])

    acc = jnp.zeros((tm, D_MODEL), F32)
    up(0)
    for f in range(nf):
        cs = pl.ds(f * FF_CHUNK, FF_CHUNK)
        if f + 1 < nf:
            up(f + 1)
        a = conv3(ua_scr.at[f % 2], cwa_ref, cba_ref, cs)
        b = conv3(ub_scr.at[f % 2], cwb_ref, cbb_ref, cs)
        acc += _dot((_gelu_tanh(a) * b).astype(BF16), wd_ref[cs, :])
    y = xm_ref[0] + acc
    o_ref[0] = y * lax.rsqrt(jnp.mean(y * y, axis=-1, keepdims=True) + EPS) * gf_ref[...]


def _ffn(x, norm2_g, w_up, conv_w, conv_b, w_down, final_g):
    B, L, D = x.shape
    tm = min(ROW_TILE, L)
    consts = (norm2_g.reshape(1, D), w_up[:, :D_FF].astype(BF16), w_up[:, D_FF:].astype(BF16),
              conv_w[:, :D_FF], conv_w[:, D_FF:], conv_b[:D_FF].reshape(1, -1), conv_b[D_FF:].reshape(1, -1),
              w_down.astype(BF16), final_g.reshape(1, D))
    return pl.pallas_call(
        functools.partial(_ffn_kernel, tm=tm),
        grid=(B, L // tm),
        in_specs=_halo_specs(tm, L, D) + [_const_spec(a.shape) for a in consts],
        out_specs=pl.BlockSpec((1, tm, D), lambda b, i: (b, i, 0)),
        out_shape=jax.ShapeDtypeStruct((B, L, D), F32),
        scratch_shapes=[pltpu.VMEM((2, tm + 2 * HALO, FF_CHUNK), F32)] * 2,
        compiler_params=_params("parallel", "arbitrary"),
        name="conv_ffn",
    )(x, x, x, *consts)


def _trunk(x, prm, spectra, fwd, inv, rot, ret_tables, n):
    hy, rt = _in_proj(x, prm["norm1_g"], prm["w_in"], prm["hy_conv_w"], prm["hy_conv_b"], *rot)
    z = _long_conv(hy, 0, hy, 1, spectra, 0, prm["hy_bias"][0], fwd, inv, n)
    z = _long_conv(z, 0, hy, 2, spectra, 1, prm["hy_bias"][1], fwd, inv, n)
    zr = _retention(rt, ret_tables, prm["ret_gn_g"], prm["ret_gn_b"])
    x = _mix(x, z, zr, prm["hy_out_g"], prm["w_out"])
    return _ffn(x, prm["norm2_g"], prm["w_up"], prm["ffn_conv_w"], prm["ffn_conv_b"], prm["w_down"],
                prm["final_g"])


def _layer(xs, prm, n=CONV_BLOCKS):
    L = xs[0].shape[1]
    assert all(x.shape[1] == L for x in xs) and L % (n * CMAC_ROWS) == 0 and L % (RET_CHUNK * RET_UNROLL) == 0
    fwd, inv = _dft_tables(L // n)
    spectra = _filter_spectra(L, n, fwd, prm["filt_w1"], prm["filt_b1"], prm["filt_w2"], prm["filt_b2"],
                              prm["filt_w3"], prm["filt_b3"], prm["filt_freq"], prm["filt_w4"])
    rot = _rotary_tables(L)
    ret_tables = _retention_tables()
    return tuple(_trunk(x, prm, spectra, fwd, inv, rot, ret_tables, n) for x in xs)


def kernel(x_prompt, x_sample, norm1_g, w_in, hy_conv_w, hy_conv_b, filt_w1, filt_b1, filt_w2, filt_b2,
           filt_w3, filt_b3, filt_freq, filt_w4, hy_bias, hy_out_g, ret_gn_g, ret_gn_b, w_out, norm2_g,
           w_up, ffn_conv_w, ffn_conv_b, w_down, final_g):
    layer = dict(norm1_g=norm1_g, w_in=w_in, hy_conv_w=hy_conv_w, hy_conv_b=hy_conv_b, filt_w1=filt_w1,
                 filt_b1=filt_b1, filt_w2=filt_w2, filt_b2=filt_b2, filt_w3=filt_w3, filt_b3=filt_b3,
                 filt_freq=filt_freq, filt_w4=filt_w4, hy_bias=hy_bias, hy_out_g=hy_out_g,
                 ret_gn_g=ret_gn_g, ret_gn_b=ret_gn_b, w_out=w_out, norm2_g=norm2_g, w_up=w_up,
                 ffn_conv_w=ffn_conv_w, ffn_conv_b=ffn_conv_b, w_down=w_down)
    assert all(v.shape[0] == 1 for v in layer.values())
    prm = {k: v[0] for k, v in layer.items()}
    prm["final_g"] = final_g
    return _layer((x_prompt, x_sample), prm)
```

```python
import functools
import math

import jax
import jax.numpy as jnp
from jax import lax
from jax.experimental import pallas as pl
from jax.experimental.pallas import tpu as pltpu

F32 = jnp.float32
BF16 = jnp.bfloat16

D_MODEL = 1024
HY_W = D_MODEL // 2
RT_W = D_MODEL - HY_W
HY_ORDER = 2
HY_COLS = (HY_ORDER + 1) * HY_W
RT_COLS = 4 * RT_W
FILTER_BANDS = 16
FILTER_HIDDEN = 64
DECAY_TARGET = 1e-2
FAST_DECAY_PCT = 0.3
SLOW_DECAY_PCT = 1.5
RT_HEADS = 8
RT_HEAD_DIM = RT_W // RT_HEADS
ROPE_BASE = 10000.0
DECAY_OFFSET_FWD = 5.0
DECAY_OFFSET_BWD = 5.5
D_FF = ((8 * D_MODEL // 3 + 127) // 128) * 128
EPS = 1e-6

LANES = 128
HALO = 8
FHALO = 16
DFT_SPLIT = 32
CONV_BLOCKS = 4
CONV_CT = 256
ROW_TILE = 512
RET_CHUNK = 128
RET_UNROLL = 2
RET_LANES = 256
CMAC_ROWS = 32
VMEM_LIMIT = 58 * 1024 * 1024


def _params(*sem):
    return pltpu.CompilerParams(dimension_semantics=sem, vmem_limit_bytes=VMEM_LIMIT)


def _dot(a, b):
    return jnp.dot(a, b, preferred_element_type=F32)


def _dot_hi(a, b):
    return jnp.dot(a, b, preferred_element_type=F32, precision=lax.Precision.HIGHEST)


def _const_spec(shape):
    nd = len(shape)
    return pl.BlockSpec(shape, lambda *_: (0,) * nd, pipeline_mode=pl.Buffered(1))


def _dft_tables(P):
    k = jnp.arange(P, dtype=jnp.int32)[:, None]

    def cs(tt):
        ang = (((2 * k + 1) * tt) % (4 * P)).astype(F32) * (math.pi / (2 * P))
        return jnp.cos(ang), jnp.sin(ang)

    ca, sa = cs(DFT_SPLIT * jnp.arange(P // DFT_SPLIT, dtype=jnp.int32)[None, :])
    cb, sb = cs(jnp.arange(DFT_SPLIT, dtype=jnp.int32)[None, :])
    cos = (ca[:, :, None] * cb[:, None, :] - sa[:, :, None] * sb[:, None, :]).reshape(P, P)
    sin = (sa[:, :, None] * cb[:, None, :] + ca[:, :, None] * sb[:, None, :]).reshape(P, P)
    fwd = jnp.concatenate([cos, -sin], axis=0).astype(BF16)
    inv = jnp.concatenate([cos.T, -sin.T], axis=1).astype(BF16)
    return fwd, inv


def _rotary_tables(L):
    inv = ROPE_BASE ** (-jnp.arange(0, RT_HEAD_DIM, 2, dtype=F32) / RT_HEAD_DIM)
    ang = jnp.arange(L, dtype=F32)[:, None] * inv[None, :]
    cos, sin = jnp.cos(ang), jnp.sin(ang)
    reps = LANES // RT_HEAD_DIM
    cos_t = jnp.tile(jnp.concatenate([cos, cos], axis=1), (1, reps))
    sin_t = jnp.tile(jnp.concatenate([-sin, sin], axis=1), (1, reps))
    return cos_t, sin_t


def _retention_tables():
    C = RET_CHUNK
    heads = jnp.arange(RT_HEADS, dtype=F32)
    log_f = jnp.log(1.0 - 2.0 ** (-DECAY_OFFSET_FWD - heads))
    log_b = jnp.log(1.0 - 2.0 ** (-DECAY_OFFSET_BWD - heads))
    idx = jnp.arange(C, dtype=F32)
    diff = idx[:, None] - idx[None, :]
    lower = diff >= 0
    d_f = jnp.where(lower[None], jnp.exp(jnp.where(lower, diff, 0.0)[None] * log_f[:, None, None]), 0.0)
    d_b = jnp.where(~lower[None], jnp.exp(jnp.where(~lower, -diff, 0.0)[None] * log_b[:, None, None]), 0.0)
    dmask = (d_f + d_b).reshape(RT_HEADS // 2, 2 * C, C)

    def lanes(tab):
        rows = tab.shape[1]
        t = jnp.repeat(tab[:, :, None], RT_HEAD_DIM, axis=2)
        t = t.reshape(RT_HEADS // 2, 2, rows, RT_HEAD_DIM).transpose(0, 2, 1, 3)
        return t.reshape(RT_HEADS // 2, rows, LANES)

    qd_f = lanes(jnp.exp((idx + 1.0)[None, :] * log_f[:, None]))
    kd_f = lanes(jnp.exp((C - 1.0 - idx)[None, :] * log_f[:, None]))
    qd_b = lanes(jnp.exp((C - idx)[None, :] * log_b[:, None]))
    kd_b = lanes(jnp.exp(idx[None, :] * log_b[:, None]))
    cd_f = lanes(jnp.exp(C * log_f)[:, None])
    cd_b = lanes(jnp.exp(C * log_b)[:, None])
    return dmask, qd_f, kd_f, qd_b, kd_b, cd_f, cd_b


def _filter_kernel(w1t_ref, w1c_ref, w1s_ref, b1_ref, w2_ref, b2_ref, w3_ref, b3_ref, fr_ref, w4_ref, band_ref,
                   delta_ref, fwd_ref, g_ref, prev_scr, tap_scr, *, P, L, n):
    step = pl.program_id(0)
    e = step - n
    pos_l = jnp.abs(e * P + lax.broadcasted_iota(jnp.int32, (1, P), 1)).astype(F32)
    ang = (2.0 * math.pi / L) * pos_l * band_ref[...]
    freq = fr_ref[...]
    h = (w1t_ref[...] * (pos_l * (1.0 / (L - 1))) + _dot_hi(w1c_ref[...], jnp.cos(ang))
         + _dot_hi(w1s_ref[...], -jnp.sin(ang)) + b1_ref[...])
    h = jnp.sin(freq * h)
    h = jnp.sin(freq * (_dot_hi(w2_ref[...], h) + b2_ref[...]))
    h = jnp.sin(freq * (_dot_hi(w3_ref[...], h) + b3_ref[...]))
    row = lax.broadcasted_iota(jnp.int32, (P, 1), 0)
    t = jnp.abs(e * P + row).astype(F32) * (1.0 / (L - 1))
    direction = (e < 0).astype(jnp.int32)
    sgn = (1 - 2 * (row & 1)).astype(F32)
    for c in range(g_ref.shape[2] // CONV_CT):
        cs = pl.ds(c * CONV_CT, CONV_CT)
        taps = lax.dot_general(h, w4_ref[pl.ds(direction, 1), :, cs][0], (((0,), (0,)), ((), ())),
                               preferred_element_type=F32, precision=lax.Precision.HIGHEST)
        taps = (taps * jnp.exp(-t * delta_ref[:, cs])).astype(BF16)
        s1 = _dot(fwd_ref[...], taps)

        @pl.when(step > 0)
        def _():
            s2_re = prev_scr[pl.ds(0, P), cs] - tap_scr[:, cs]
            s2_im = prev_scr[pl.ds(P, P), cs]
            g_ref[0, pl.ds(0, P), cs] = s1[:P] - sgn * s2_im
            g_ref[0, pl.ds(P, P), cs] = s1[P:] + sgn * s2_re

        prev_scr[:, cs] = s1
        tap_scr[:, cs] = taps[0:1].astype(F32)


def _filter_spectra(L, n, fwd, w1, b1, w2, b2, w3, b3, freq, w4):
    P = L // n
    cols = HY_ORDER * HY_W
    bands = jnp.linspace(1e-4, FILTER_BANDS - 1, FILTER_BANDS, dtype=F32)
    deltas = jnp.abs(jnp.linspace(math.log(DECAY_TARGET) / FAST_DECAY_PCT,
                                  math.log(DECAY_TARGET) / SLOW_DECAY_PCT, HY_W, dtype=F32))
    delta = jnp.tile(deltas, HY_ORDER)[None, :]
    w4r = w4.reshape(FILTER_HIDDEN, 2, cols).transpose(1, 0, 2)
    col = lambda a: a.reshape(-1, 1)
    args = (col(w1[0]), w1[1:1 + FILTER_BANDS].T, w1[1 + FILTER_BANDS:].T, col(b1), w2.T, col(b2), w3.T, col(b3),
            col(freq), w4r, col(bands), delta, fwd)
    return pl.pallas_call(
        functools.partial(_filter_kernel, P=P, L=L, n=n),
        grid=(2 * n,),
        in_specs=[_const_spec(a.shape) for a in args],
        out_specs=pl.BlockSpec((1, 2 * P, cols), lambda s: (jnp.maximum(s - 1, 0), 0, 0)),
        out_shape=jax.ShapeDtypeStruct((2 * n - 1, 2 * P, cols), F32),
        scratch_shapes=[pltpu.VMEM((2 * P, cols), F32), pltpu.VMEM((1, cols), F32)],
        compiler_params=_params("arbitrary"),
        name="hyena_filter",
    )(*args)


def _in_kernel(xp_ref, xm_ref, xn_ref, g1_ref, wh_ref, wr_ref, cw_ref, cb_ref, cos_ref, sin_ref,
               hy_ref, rt_ref, pe_scr, *, tm):
    i = pl.program_id(1)
    last = pl.num_programs(1) - 1
    xe = jnp.concatenate([xp_ref[0], xm_ref[0], xn_ref[0]], axis=0)
    xe = xe * lax.rsqrt(jnp.mean(xe * xe, axis=-1, keepdims=True) + EPS) * g1_ref[...]
    row = lax.broadcasted_iota(jnp.int32, (tm + 2 * HALO, 1), 0)
    outside = ((row < HALO) & (i == 0)) | ((row >= tm + HALO) & (i == last))
    xe = jnp.where(outside, 0.0, xe)
    pe_scr[...] = _dot(xe.astype(BF16), wh_ref[...])
    hy = (pe_scr[pl.ds(HALO - 1, tm), :] * cw_ref[0:1, :] + pe_scr[pl.ds(HALO, tm), :] * cw_ref[1:2, :]
          + pe_scr[pl.ds(HALO + 1, tm), :] * cw_ref[2:3, :] + cb_ref[...])
    hy_ref[0] = hy.astype(BF16)

    pr = _dot(xe[HALO:HALO + tm].astype(BF16), wr_ref[...])
    cos = cos_ref[...]
    sin = sin_ref[...]
    lane = lax.broadcasted_iota(jnp.int32, (1, LANES), 1)
    first_half = (lane & (RT_HEAD_DIM // 2)) == 0
    for j in range(2 * RT_W // LANES):
        xg = pr[:, j * LANES:(j + 1) * LANES]
        partner = jnp.where(first_half, pltpu.roll(xg, LANES - RT_HEAD_DIM // 2, 1),
                            pltpu.roll(xg, RT_HEAD_DIM // 2, 1))
        r = xg * cos + partner * sin
        if j >= RT_W // LANES:
            r = r * RT_HEAD_DIM ** -0.5
        rt_ref[0, :, j * LANES:(j + 1) * LANES] = r.astype(BF16)
    rt_ref[0, :, 2 * RT_W:] = pr[:, 2 * RT_W:].astype(BF16)


def _halo_specs(tm, L, D):
    blocks = tm // HALO
    prev = pl.BlockSpec((1, HALO, D), lambda b, i: (b, jnp.maximum(i * blocks - 1, 0), 0))
    main = pl.BlockSpec((1, tm, D), lambda b, i: (b, i, 0))
    nxt = pl.BlockSpec((1, HALO, D), lambda b, i: (b, jnp.minimum((i + 1) * blocks, L // HALO - 1), 0))
    return [prev, main, nxt]


def _in_proj(x, g1, w_in, cw, cb, cos_t, sin_t):
    B, L, D = x.shape
    tm = min(ROW_TILE, L)
    wh = w_in[:, :HY_COLS].astype(BF16)
    wr = w_in[:, HY_COLS:].astype(BF16)
    consts = (g1.reshape(1, D), wh, wr, cw, cb.reshape(1, -1))
    return pl.pallas_call(
        functools.partial(_in_kernel, tm=tm),
        grid=(B, L // tm),
        in_specs=_halo_specs(tm, L, D) + [_const_spec(a.shape) for a in consts]
        + [pl.BlockSpec((tm, LANES), lambda b, i: (i, 0))] * 2,
        out_specs=[pl.BlockSpec((1, tm, HY_COLS), lambda b, i: (b, i, 0)),
                   pl.BlockSpec((1, tm, RT_COLS), lambda b, i: (b, i, 0))],
        out_shape=[jax.ShapeDtypeStruct((B, L, HY_COLS), BF16),
                   jax.ShapeDtypeStruct((B, L, RT_COLS), BF16)],
        scratch_shapes=[pltpu.VMEM((tm + 2 * HALO, HY_COLS), F32)],
        compiler_params=_params("parallel", "arbitrary"),
        name="in_proj",
    )(x, x, x, *consts, cos_t, sin_t)


def _conv_kernel(u_ref, x_ref, g_ref, bias_ref, fwd_ref, inv_ref, o_ref, spec_scr, prod_scr, *, P, n):
    for j in range(n):
        spec_scr[j] = _dot(fwd_ref[...], u_ref[0, pl.ds(j * P, P), :])
    for i in range(n):
        def rows(t, carry, i=i):
            r_re = pl.ds(pl.multiple_of(t * CMAC_ROWS, CMAC_ROWS), CMAC_ROWS)
            r_im = pl.ds(pl.multiple_of(P + t * CMAC_ROWS, CMAC_ROWS), CMAC_ROWS)
            acc_re = jnp.zeros((CMAC_ROWS, CONV_CT), F32)
            acc_im = jnp.zeros((CMAC_ROWS, CONV_CT), F32)
            for j in range(n):
                d = i - j + n - 1
                g_re, g_im = g_ref[d, r_re, :], g_ref[d, r_im, :]
                u_re, u_im = spec_scr[j, r_re, :], spec_scr[j, r_im, :]
                acc_re += g_re * u_re - g_im * u_im
                acc_im += g_re * u_im + g_im * u_re
            prod_scr[i, r_re, :] = acc_re.astype(BF16)
            prod_scr[i, r_im, :] = acc_im.astype(BF16)
            return carry
        lax.fori_loop(0, P // CMAC_ROWS, rows, 0)
    for i in range(n):
        y = _dot(inv_ref[...], prod_scr[i]) * (1.0 / P)
        blk = pl.ds(i * P, P)
        u = u_ref[0, blk, :].astype(F32)
        o_ref[0, blk, :] = (x_ref[0, blk, :].astype(F32) * (y + u * bias_ref[...])).astype(BF16)


def _long_conv(u, u_col, gate, gate_col, spectra, order, bias, fwd, inv, n):
    B, L = u.shape[:2]
    P = L // n
    nct = HY_W // CONV_CT
    return pl.pallas_call(
        functools.partial(_conv_kernel, P=P, n=n),
        grid=(nct, B),
        in_specs=[pl.BlockSpec((1, L, CONV_CT), lambda c, b: (b, 0, u_col * nct + c)),
                  pl.BlockSpec((1, L, CONV_CT), lambda c, b: (b, 0, gate_col * nct + c)),
                  pl.BlockSpec((2 * n - 1, 2 * P, CONV_CT), lambda c, b: (0, 0, order * nct + c),
                               pipeline_mode=pl.Buffered(1)),
                  pl.BlockSpec((1, CONV_CT), lambda c, b: (0, c)),
                  _const_spec(fwd.shape), _const_spec(inv.shape)],
        out_specs=pl.BlockSpec((1, L, CONV_CT), lambda c, b: (b, 0, c)),
        out_shape=jax.ShapeDtypeStruct((B, L, HY_W), BF16),
        scratch_shapes=[pltpu.VMEM((n, 2 * P, CONV_CT), F32), pltpu.VMEM((n, 2 * P, CONV_CT), BF16)],
        compiler_params=_params("arbitrary", "arbitrary"),
        name=f"long_conv{order}",
    )(u, gate, spectra, bias.reshape(1, -1), fwd, inv)


def _ret_kernel(q_ref, k_ref, v_ref, g_ref, dm_ref, qdf_ref, kdf_ref, qdb_ref, kdb_ref, cdf_ref, cdb_ref,
                avg_ref, gng_ref, gnb_ref, o_ref, acc_scr, *, L):
    C = RET_CHUNK
    pairs = RET_LANES // LANES
    nchunk = L // C
    lane = lax.broadcasted_iota(jnp.int32, (1, LANES), 1)
    head_a = lane < RT_HEAD_DIM
    ri = lax.broadcasted_iota(jnp.int32, (LANES, LANES), 0)
    ci = lax.broadcasted_iota(jnp.int32, (LANES, LANES), 1)
    same_head = ((ri < RT_HEAD_DIM) == (ci < RT_HEAD_DIM)).astype(F32)
    zero = jnp.zeros((), BF16)

    def split_heads(a):
        return jnp.concatenate([jnp.where(head_a, a, zero), jnp.where(head_a, zero, a)], axis=0)

    def state_update(s, kc, vc, kd, cd):
        kt = (kc.astype(F32) * kd).T.astype(BF16)
        return s * cd + _dot(kt, vc) * same_head

    def fwd_body(nidx, states):
        rows = pl.ds(pl.multiple_of(nidx * C, C), C)
        new = []
        for p in range(pairs):
            ls = slice(p * LANES, (p + 1) * LANES)
            qc, kc, vc = q_ref[0, rows, ls], k_ref[0, rows, ls], v_ref[0, rows, ls]
            s = lax.dot_general(split_heads(qc), kc, (((1,), (1,)), ((), ())), preferred_element_type=F32)
            s = (s * dm_ref[p]).astype(BF16)
            qd = (qc.astype(F32) * qdf_ref[p]).astype(BF16)
            lhs = jnp.concatenate([s[:C], s[C:], qd], axis=1)
            rhs = jnp.concatenate([split_heads(vc), states[p].astype(BF16)], axis=0)
            acc_scr[rows, ls] = _dot(lhs, rhs)
            new.append(state_update(states[p], kc, vc, kdf_ref[p], cdf_ref[p]))
        return tuple(new)

    def bwd_body(it, states):
        rows = pl.ds(pl.multiple_of((nchunk - 1 - it) * C, C), C)
        new = []
        for p in range(pairs):
            ls = slice(p * LANES, (p + 1) * LANES)
            qc, kc, vc = q_ref[0, rows, ls], k_ref[0, rows, ls], v_ref[0, rows, ls]
            acc_scr[rows, ls] += _dot((qc.astype(F32) * qdb_ref[p]).astype(BF16), states[p].astype(BF16))
            new.append(state_update(states[p], kc, vc, kdb_ref[p], cdb_ref[p]))
        return tuple(new)

    init = tuple(jnp.zeros((LANES, LANES), F32) for _ in range(pairs))
    lax.fori_loop(0, nchunk, fwd_body, init, unroll=RET_UNROLL)
    lax.fori_loop(0, nchunk, bwd_body, init, unroll=RET_UNROLL)

    tn = min(ROW_TILE, L)
    avg = avg_ref[...]

    def group_mean(a):
        hi = a.astype(BF16)
        lo = (a - hi.astype(F32)).astype(BF16)
        return _dot(hi, avg) + _dot(lo, avg)

    def norm_body(it, carry):
        rows = pl.ds(pl.multiple_of(it * tn, tn), tn)
        o = acc_scr[rows, :]
        oc = o - group_mean(o)
        var = group_mean(oc * oc)
        on = oc * lax.rsqrt(var + EPS) * gng_ref[...] + gnb_ref[...]
        g = g_ref[0, rows, :].astype(F32)
        o_ref[0, rows, :] = (on * g * (1.0 / (1.0 + jnp.exp(-g)))).astype(BF16)
        return carry

    lax.fori_loop(0, L // tn, norm_body, 0)


def _retention(rt, tables, gn_g, gn_b):
    B, L, _ = rt.shape
    groups = RT_W // RET_LANES
    pairs = RET_LANES // LANES
    li = jnp.arange(RET_LANES) // RT_HEAD_DIM
    avg = ((li[:, None] == li[None, :]).astype(F32) / RT_HEAD_DIM).astype(BF16)
    col = lambda which: pl.BlockSpec((1, L, RET_LANES), lambda g, b: (b, 0, which * groups + g))
    tab = lambda a: pl.BlockSpec((pairs,) + a.shape[1:], lambda g, b: (g, 0, 0))
    vec = pl.BlockSpec((1, RET_LANES), lambda g, b: (0, g))
    return pl.pallas_call(
        functools.partial(_ret_kernel, L=L),
        grid=(groups, B),
        in_specs=[col(0), col(1), col(2), col(3)] + [tab(a) for a in tables] + [_const_spec(avg.shape), vec, vec],
        out_specs=pl.BlockSpec((1, L, RET_LANES), lambda g, b: (b, 0, g)),
        out_shape=jax.ShapeDtypeStruct((B, L, RT_W), BF16),
        scratch_shapes=[pltpu.VMEM((L, RET_LANES), F32)],
        compiler_params=_params("arbitrary", "arbitrary"),
        name="retention",
    )(rt, rt, rt, rt, *tables, avg, gn_g.reshape(1, -1), gn_b.reshape(1, -1))


def _gelu_tanh(a):
    return 0.5 * a * (1.0 + jnp.tanh(math.sqrt(2.0 / math.pi) * (a + 0.044715 * (a * a * a))))


def _rms(x, g):
    return x * lax.rsqrt(jnp.mean(x * x, axis=-1, keepdims=True) + EPS) * g


def _ffn_kernel(xp_ref, xm_ref, xn_ref, hp_ref, hm_ref, hn_ref, rp_ref, rm_ref, rn_ref, hg_ref, woh_ref, wor_ref,
                g2_ref, wa_ref, wb_ref, cwa_ref, cwb_ref, cba_ref, cbb_ref, wd_ref, gf_ref, o_ref,
                ua_scr, ub_scr, *, tm):
    i = pl.program_id(1)
    last = pl.num_programs(1) - 1
    cat = lambda p, m, n: jnp.concatenate([p[0], m[0], n[0]], axis=0)
    yh = _rms(cat(hp_ref, hm_ref, hn_ref).astype(F32), hg_ref[...]).astype(BF16)
    x1 = cat(xp_ref, xm_ref, xn_ref) + _dot(yh, woh_ref[...]) + _dot(cat(rp_ref, rm_ref, rn_ref), wor_ref[...])
    xe = _rms(x1, g2_ref[...])
    row = lax.broadcasted_iota(jnp.int32, (tm + 2 * FHALO, 1), 0)
    outside = ((row < FHALO) & (i == 0)) | ((row >= tm + FHALO) & (i == last))
    xe = jnp.where(outside, 0.0, xe).astype(BF16)

    def conv3(scr, cw_ref, cb_ref):
        return (scr[pl.ds(FHALO - 1, tm), :] * cw_ref[0:1, :] + scr[pl.ds(FHALO, tm), :] * cw_ref[1:2, :]
                + scr[pl.ds(FHALO + 1, tm), :] * cw_ref[2:3, :] + cb_ref[...])

    ua_scr[...] = _dot(xe, wa_ref[...])
    ub_scr[...] = _dot(xe, wb_ref[...])
    gated = _gelu_tanh(conv3(ua_scr, cwa_ref, cba_ref)) * conv3(ub_scr, cwb_ref, cbb_ref)
    y = x1[FHALO:FHALO + tm] + _dot(gated.astype(BF16), wd_ref[...])
    o_ref[0] = _rms(y, gf_ref[...])


def _mix_ffn(x, zh, zr, hy_out_g, w_out, norm2_g, w_up, conv_w, conv_b, w_down, final_g):
    B, L, D = x.shape
    tm = min(ROW_TILE, L)
    blocks = tm // FHALO

    def halo3(w):
        prev = pl.BlockSpec((1, FHALO, w), lambda b, i: (b, jnp.maximum(i * blocks - 1, 0), 0))
        main = pl.BlockSpec((1, tm, w), lambda b, i: (b, i, 0))
        nxt = pl.BlockSpec((1, FHALO, w), lambda b, i: (b, jnp.minimum((i + 1) * blocks, L // FHALO - 1), 0))
        return [prev, main, nxt]

    consts = (hy_out_g.reshape(1, -1), w_out[:HY_W].astype(BF16), w_out[HY_W:].astype(BF16),
              norm2_g.reshape(1, D), w_up[:, :D_FF].astype(BF16), w_up[:, D_FF:].astype(BF16),
              conv_w[:, :D_FF], conv_w[:, D_FF:], conv_b[:D_FF].reshape(1, -1), conv_b[D_FF:].reshape(1, -1),
              w_down.astype(BF16), final_g.reshape(1, D))
    return pl.pallas_call(
        functools.partial(_ffn_kernel, tm=tm),
        grid=(B, L // tm),
        in_specs=halo3(D) + halo3(HY_W) + halo3(RT_W) + [_const_spec(a.shape) for a in consts],
        out_specs=pl.BlockSpec((1, tm, D), lambda b, i: (b, i, 0)),
        out_shape=jax.ShapeDtypeStruct((B, L, D), F32),
        scratch_shapes=[pltpu.VMEM((tm + 2 * FHALO, D_FF), F32)] * 2,
        compiler_params=_params("parallel", "arbitrary"),
        name="mix_ffn",
    )(x, x, x, zh, zh, zh, zr, zr, zr, *consts)


def _trunk(x, prm, spectra, fwd, inv, rot, ret_tables, n):
    hy, rt = _in_proj(x, prm["norm1_g"], prm["w_in"], prm["hy_conv_w"], prm["hy_conv_b"], *rot)
    z = _long_conv(hy, 0, hy, 1, spectra, 0, prm["hy_bias"][0], fwd, inv, n)
    z = _long_conv(z, 0, hy, 2, spectra, 1, prm["hy_bias"][1], fwd, inv, n)
    zr = _retention(rt, ret_tables, prm["ret_gn_g"], prm["ret_gn_b"])
    return _mix_ffn(x, z, zr, prm["hy_out_g"], prm["w_out"], prm["norm2_g"], prm["w_up"], prm["ffn_conv_w"],
                    prm["ffn_conv_b"], prm["w_down"], prm["final_g"])


def _layer(xs, prm, n=CONV_BLOCKS):
    L = xs[0].shape[1]
    assert all(x.shape[1] == L for x in xs) and L % (n * CMAC_ROWS) == 0 and L % (RET_CHUNK * RET_UNROLL) == 0
    fwd, inv = _dft_tables(L // n)
    spectra = _filter_spectra(L, n, fwd, prm["filt_w1"], prm["filt_b1"], prm["filt_w2"], prm["filt_b2"],
                              prm["filt_w3"], prm["filt_b3"], prm["filt_freq"], prm["filt_w4"])
    rot = _rotary_tables(L)
    ret_tables = _retention_tables()
    return tuple(_trunk(x, prm, spectra, fwd, inv, rot, ret_tables, n) for x in xs)


def kernel(x_prompt, x_sample, norm1_g, w_in, hy_conv_w, hy_conv_b, filt_w1, filt_b1, filt_w2, filt_b2,
           filt_w3, filt_b3, filt_freq, filt_w4, hy_bias, hy_out_g, ret_gn_g, ret_gn_b, w_out, norm2_g,
           w_up, ffn_conv_w, ffn_conv_b, w_down, final_g):
    layer = dict(norm1_g=norm1_g, w_in=w_in, hy_conv_w=hy_conv_w, hy_conv_b=hy_conv_b, filt_w1=filt_w1,
                 filt_b1=filt_b1, filt_w2=filt_w2, filt_b2=filt_b2, filt_w3=filt_w3, filt_b3=filt_b3,
                 filt_freq=filt_freq, filt_w4=filt_w4, hy_bias=hy_bias, hy_out_g=hy_out_g,
                 ret_gn_g=ret_gn_g, ret_gn_b=ret_gn_b, w_out=w_out, norm2_g=norm2_g, w_up=w_up,
                 ffn_conv_w=ffn_conv_w, ffn_conv_b=ffn_conv_b, w_down=w_down)
    assert all(v.shape[0] == 1 for v in layer.values())
    prm = {k: v[0] for k, v in layer.items()}
    prm["final_g"] = final_g
    return _layer((x_prompt, x_sample), prm)
```

```python
import functools
import math

import jax
import jax.numpy as jnp
from jax import lax
from jax.experimental import pallas as pl
from jax.experimental.pallas import tpu as pltpu

F32 = jnp.float32
BF16 = jnp.bfloat16

D_MODEL = 1024
HY_W = D_MODEL // 2
RT_W = D_MODEL - HY_W
HY_ORDER = 2
HY_COLS = (HY_ORDER + 1) * HY_W
RT_COLS = 4 * RT_W
FILTER_BANDS = 16
FILTER_HIDDEN = 64
DECAY_TARGET = 1e-2
FAST_DECAY_PCT = 0.3
SLOW_DECAY_PCT = 1.5
RT_HEADS = 8
RT_HEAD_DIM = RT_W // RT_HEADS
ROPE_BASE = 10000.0
DECAY_OFFSET_FWD = 5.0
DECAY_OFFSET_BWD = 5.5
D_FF = ((8 * D_MODEL // 3 + 127) // 128) * 128
EPS = 1e-6

LANES = 128
HALO = 8
FHALO = 16
DFT_SPLIT = 32
CONV_BLOCKS = 4
CONV_CT = 256
ROW_TILE = 512
RET_CHUNK = 128
RET_UNROLL = 2
RET_LANES = 256
CMAC_ROWS = 32
VMEM_LIMIT = 58 * 1024 * 1024


def _params(*sem):
    return pltpu.CompilerParams(dimension_semantics=sem, vmem_limit_bytes=VMEM_LIMIT)


def _dot(a, b):
    return jnp.dot(a, b, preferred_element_type=F32)


def _dot_hi(a, b):
    return jnp.dot(a, b, preferred_element_type=F32, precision=lax.Precision.HIGHEST)


def _const_spec(shape):
    nd = len(shape)
    return pl.BlockSpec(shape, lambda *_: (0,) * nd, pipeline_mode=pl.Buffered(1))


def _dft_tables(P):
    k = jnp.arange(P, dtype=jnp.int32)[:, None]

    def cs(tt):
        ang = (((2 * k + 1) * tt) % (4 * P)).astype(F32) * (math.pi / (2 * P))
        return jnp.cos(ang), jnp.sin(ang)

    ca, sa = cs(DFT_SPLIT * jnp.arange(P // DFT_SPLIT, dtype=jnp.int32)[None, :])
    cb, sb = cs(jnp.arange(DFT_SPLIT, dtype=jnp.int32)[None, :])
    cos = (ca[:, :, None] * cb[:, None, :] - sa[:, :, None] * sb[:, None, :]).reshape(P, P)
    sin = (sa[:, :, None] * cb[:, None, :] + ca[:, :, None] * sb[:, None, :]).reshape(P, P)
    fwd = jnp.concatenate([cos, -sin], axis=0).astype(BF16)
    inv = jnp.concatenate([cos.T, -sin.T], axis=1).astype(BF16)
    return fwd, inv


def _rotary_tables(L):
    inv = ROPE_BASE ** (-jnp.arange(0, RT_HEAD_DIM, 2, dtype=F32) / RT_HEAD_DIM)
    ang = jnp.arange(L, dtype=F32)[:, None] * inv[None, :]
    cos, sin = jnp.cos(ang), jnp.sin(ang)
    reps = LANES // RT_HEAD_DIM
    cos_t = jnp.tile(jnp.concatenate([cos, cos], axis=1), (1, reps))
    sin_t = jnp.tile(jnp.concatenate([-sin, sin], axis=1), (1, reps))
    return cos_t, sin_t


def _retention_tables():
    C = RET_CHUNK
    heads = jnp.arange(RT_HEADS, dtype=F32)
    log_f = jnp.log(1.0 - 2.0 ** (-DECAY_OFFSET_FWD - heads))
    log_b = jnp.log(1.0 - 2.0 ** (-DECAY_OFFSET_BWD - heads))
    idx = jnp.arange(C, dtype=F32)
    diff = idx[:, None] - idx[None, :]
    lower = diff >= 0
    d_f = jnp.where(lower[None], jnp.exp(jnp.where(lower, diff, 0.0)[None] * log_f[:, None, None]), 0.0)
    d_b = jnp.where(~lower[None], jnp.exp(jnp.where(~lower, -diff, 0.0)[None] * log_b[:, None, None]), 0.0)
    dmask = (d_f + d_b).reshape(RT_HEADS // 2, 2 * C, C)

    def lanes(tab):
        rows = tab.shape[1]
        t = jnp.repeat(tab[:, :, None], RT_HEAD_DIM, axis=2)
        t = t.reshape(RT_HEADS // 2, 2, rows, RT_HEAD_DIM).transpose(0, 2, 1, 3)
        return t.reshape(RT_HEADS // 2, rows, LANES)

    qd_f = lanes(jnp.exp((idx + 1.0)[None, :] * log_f[:, None]))
    kd_f = lanes(jnp.exp((C - 1.0 - idx)[None, :] * log_f[:, None]))
    qd_b = lanes(jnp.exp((C - idx)[None, :] * log_b[:, None]))
    kd_b = lanes(jnp.exp(idx[None, :] * log_b[:, None]))
    cd_f = lanes(jnp.exp(C * log_f)[:, None])
    cd_b = lanes(jnp.exp(C * log_b)[:, None])
    return dmask, qd_f, kd_f, qd_b, kd_b, cd_f, cd_b


def _filter_kernel(w1t_ref, w1c_ref, w1s_ref, b1_ref, w2_ref, b2_ref, w3_ref, b3_ref, fr_ref, w4_ref, band_ref,
                   delta_ref, fwd_ref, g_ref, prev_scr, tap_scr, *, P, L, n):
    step = pl.program_id(0)
    e = step - n
    pos_l = jnp.abs(e * P + lax.broadcasted_iota(jnp.int32, (1, P), 1)).astype(F32)
    ang = (2.0 * math.pi / L) * pos_l * band_ref[...]
    freq = fr_ref[...]
    h = (w1t_ref[...] * (pos_l * (1.0 / (L - 1))) + _dot_hi(w1c_ref[...], jnp.cos(ang))
         + _dot_hi(w1s_ref[...], -jnp.sin(ang)) + b1_ref[...])
    h = jnp.sin(freq * h)
    h = jnp.sin(freq * (_dot_hi(w2_ref[...], h) + b2_ref[...]))
    h = jnp.sin(freq * (_dot_hi(w3_ref[...], h) + b3_ref[...]))
    row = lax.broadcasted_iota(jnp.int32, (P, 1), 0)
    t = jnp.abs(e * P + row).astype(F32) * (1.0 / (L - 1))
    direction = (e < 0).astype(jnp.int32)
    sgn = (1 - 2 * (row & 1)).astype(F32)
    for c in range(g_ref.shape[2] // CONV_CT):
        cs = pl.ds(c * CONV_CT, CONV_CT)
        taps = lax.dot_general(h, w4_ref[pl.ds(direction, 1), :, cs][0], (((0,), (0,)), ((), ())),
                               preferred_element_type=F32, precision=lax.Precision.HIGHEST)
        taps = (taps * jnp.exp(-t * delta_ref[:, cs])).astype(BF16)
        s1 = _dot(fwd_ref[...], taps)

        @pl.when(step > 0)
        def _():
            s2_re = prev_scr[pl.ds(0, P), cs] - tap_scr[:, cs]
            s2_im = prev_scr[pl.ds(P, P), cs]
            g_ref[0, pl.ds(0, P), cs] = s1[:P] - sgn * s2_im
            g_ref[0, pl.ds(P, P), cs] = s1[P:] + sgn * s2_re

        prev_scr[:, cs] = s1
        tap_scr[:, cs] = taps[0:1].astype(F32)


def _filter_spectra(L, n, fwd, w1, b1, w2, b2, w3, b3, freq, w4):
    P = L // n
    cols = HY_ORDER * HY_W
    bands = jnp.linspace(1e-4, FILTER_BANDS - 1, FILTER_BANDS, dtype=F32)
    deltas = jnp.abs(jnp.linspace(math.log(DECAY_TARGET) / FAST_DECAY_PCT,
                                  math.log(DECAY_TARGET) / SLOW_DECAY_PCT, HY_W, dtype=F32))
    delta = jnp.tile(deltas, HY_ORDER)[None, :]
    w4r = w4.reshape(FILTER_HIDDEN, 2, cols).transpose(1, 0, 2)
    col = lambda a: a.reshape(-1, 1)
    args = (col(w1[0]), w1[1:1 + FILTER_BANDS].T, w1[1 + FILTER_BANDS:].T, col(b1), w2.T, col(b2), w3.T, col(b3),
            col(freq), w4r, col(bands), delta, fwd)
    return pl.pallas_call(
        functools.partial(_filter_kernel, P=P, L=L, n=n),
        grid=(2 * n,),
        in_specs=[_const_spec(a.shape) for a in args],
        out_specs=pl.BlockSpec((1, 2 * P, cols), lambda s: (jnp.maximum(s - 1, 0), 0, 0)),
        out_shape=jax.ShapeDtypeStruct((2 * n - 1, 2 * P, cols), F32),
        scratch_shapes=[pltpu.VMEM((2 * P, cols), F32), pltpu.VMEM((1, cols), F32)],
        compiler_params=_params("arbitrary"),
        name="hyena_filter",
    )(*args)


def _in_kernel(xp_ref, xm_ref, xn_ref, g1_ref, wh_ref, wr_ref, cw_ref, cb_ref, cos_ref, sin_ref,
               hy_ref, rt_ref, *, tm):
    i = pl.program_id(1)
    last = pl.num_programs(1) - 1
    xe = jnp.concatenate([xm_ref[0], xn_ref[0], xp_ref[0]], axis=0)
    xe = xe * lax.rsqrt(jnp.mean(xe * xe, axis=-1, keepdims=True) + EPS) * g1_ref[...]
    row = lax.broadcasted_iota(jnp.int32, (tm + 2 * HALO, 1), 0)
    outside = ((row >= tm + HALO) & (i == 0)) | ((row >= tm) & (row < tm + HALO) & (i == last))
    xe = jnp.where(outside, 0.0, xe).astype(BF16)
    pe = _dot(xe, wh_ref[...])
    hy = (pltpu.roll(pe, 1, 0)[:tm] * cw_ref[0:1, :] + pe[:tm] * cw_ref[1:2, :]
          + pltpu.roll(pe, tm + 2 * HALO - 1, 0)[:tm] * cw_ref[2:3, :] + cb_ref[...])
    hy_ref[0] = hy.astype(BF16)

    pr = _dot(xe[:tm], wr_ref[...])
    cos = cos_ref[...]
    sin = sin_ref[...]
    lane = lax.broadcasted_iota(jnp.int32, (1, LANES), 1)
    first_half = (lane & (RT_HEAD_DIM // 2)) == 0
    for j in range(2 * RT_W // LANES):
        xg = pr[:, j * LANES:(j + 1) * LANES]
        partner = jnp.where(first_half, pltpu.roll(xg, LANES - RT_HEAD_DIM // 2, 1),
                            pltpu.roll(xg, RT_HEAD_DIM // 2, 1))
        r = xg * cos + partner * sin
        if j >= RT_W // LANES:
            r = r * RT_HEAD_DIM ** -0.5
        rt_ref[0, :, j * LANES:(j + 1) * LANES] = r.astype(BF16)
    rt_ref[0, :, 2 * RT_W:] = pr[:, 2 * RT_W:].astype(BF16)


def _halo_specs(tm, L, D):
    blocks = tm // HALO
    prev = pl.BlockSpec((1, HALO, D), lambda b, i: (b, jnp.maximum(i * blocks - 1, 0), 0))
    main = pl.BlockSpec((1, tm, D), lambda b, i: (b, i, 0))
    nxt = pl.BlockSpec((1, HALO, D), lambda b, i: (b, jnp.minimum((i + 1) * blocks, L // HALO - 1), 0))
    return [prev, main, nxt]


def _in_proj(x, g1, w_in, cw, cb, cos_t, sin_t):
    B, L, D = x.shape
    tm = min(ROW_TILE, L)
    wh = w_in[:, :HY_COLS].astype(BF16)
    wr = w_in[:, HY_COLS:].astype(BF16)
    consts = (g1.reshape(1, D), wh, wr, cw, cb.reshape(1, -1))
    return pl.pallas_call(
        functools.partial(_in_kernel, tm=tm),
        grid=(B, L // tm),
        in_specs=_halo_specs(tm, L, D) + [_const_spec(a.shape) for a in consts]
        + [pl.BlockSpec((tm, LANES), lambda b, i: (i, 0))] * 2,
        out_specs=[pl.BlockSpec((1, tm, HY_COLS), lambda b, i: (b, i, 0)),
                   pl.BlockSpec((1, tm, RT_COLS), lambda b, i: (b, i, 0))],
        out_shape=[jax.ShapeDtypeStruct((B, L, HY_COLS), BF16),
                   jax.ShapeDtypeStruct((B, L, RT_COLS), BF16)],
        compiler_params=_params("parallel", "arbitrary"),
        name="in_proj",
    )(x, x, x, *consts, cos_t, sin_t)


def _conv_kernel(u_ref, x_ref, g_ref, bias_ref, fwd_ref, inv_ref, o_ref, spec_scr, prod_scr, *, P, n):
    for j in range(n):
        spec_scr[j] = _dot(fwd_ref[...], u_ref[0, pl.ds(j * P, P), :])
    for i in range(n):
        def rows(t, carry, i=i):
            r_re = pl.ds(pl.multiple_of(t * CMAC_ROWS, CMAC_ROWS), CMAC_ROWS)
            r_im = pl.ds(pl.multiple_of(P + t * CMAC_ROWS, CMAC_ROWS), CMAC_ROWS)
            acc_re = jnp.zeros((CMAC_ROWS, CONV_CT), F32)
            acc_im = jnp.zeros((CMAC_ROWS, CONV_CT), F32)
            for j in range(n):
                d = i - j + n - 1
                g_re, g_im = g_ref[d, r_re, :], g_ref[d, r_im, :]
                u_re, u_im = spec_scr[j, r_re, :], spec_scr[j, r_im, :]
                acc_re += g_re * u_re - g_im * u_im
                acc_im += g_re * u_im + g_im * u_re
            prod_scr[i, r_re, :] = acc_re.astype(BF16)
            prod_scr[i, r_im, :] = acc_im.astype(BF16)
            return carry
        lax.fori_loop(0, P // CMAC_ROWS, rows, 0)
    for i in range(n):
        y = _dot(inv_ref[...], prod_scr[i]) * (1.0 / P)
        blk = pl.ds(i * P, P)
        u = u_ref[0, blk, :].astype(F32)
        o_ref[0, blk, :] = (x_ref[0, blk, :].astype(F32) * (y + u * bias_ref[...])).astype(BF16)


def _long_conv(u, u_col, gate, gate_col, spectra, order, bias, fwd, inv, n):
    B, L = u.shape[:2]
    P = L // n
    nct = HY_W // CONV_CT
    return pl.pallas_call(
        functools.partial(_conv_kernel, P=P, n=n),
        grid=(nct, B),
        in_specs=[pl.BlockSpec((1, L, CONV_CT), lambda c, b: (b, 0, u_col * nct + c)),
                  pl.BlockSpec((1, L, CONV_CT), lambda c, b: (b, 0, gate_col * nct + c)),
                  pl.BlockSpec((2 * n - 1, 2 * P, CONV_CT), lambda c, b: (0, 0, order * nct + c),
                               pipeline_mode=pl.Buffered(1)),
                  pl.BlockSpec((1, CONV_CT), lambda c, b: (0, c)),
                  _const_spec(fwd.shape), _const_spec(inv.shape)],
        out_specs=pl.BlockSpec((1, L, CONV_CT), lambda c, b: (b, 0, c)),
        out_shape=jax.ShapeDtypeStruct((B, L, HY_W), BF16),
        scratch_shapes=[pltpu.VMEM((n, 2 * P, CONV_CT), F32), pltpu.VMEM((n, 2 * P, CONV_CT), BF16)],
        compiler_params=_params("arbitrary", "arbitrary"),
        name=f"long_conv{order}",
    )(u, gate, spectra, bias.reshape(1, -1), fwd, inv)


def _ret_kernel(q_ref, k_ref, v_ref, g_ref, dm_ref, qdf_ref, kdf_ref, qdb_ref, kdb_ref, cdf_ref, cdb_ref,
                avg_ref, gng_ref, gnb_ref, o_ref, acc_scr, kv_scr, sb_scr, *, L):
    C = RET_CHUNK
    pairs = RET_LANES // LANES
    nchunk = L // C
    lane = lax.broadcasted_iota(jnp.int32, (1, LANES), 1)
    head_a = lane < RT_HEAD_DIM
    ri = lax.broadcasted_iota(jnp.int32, (LANES, LANES), 0)
    ci = lax.broadcasted_iota(jnp.int32, (LANES, LANES), 1)
    same_head = ((ri < RT_HEAD_DIM) == (ci < RT_HEAD_DIM)).astype(F32)
    zero = jnp.zeros((), BF16)

    def split_heads(a):
        return jnp.concatenate([jnp.where(head_a, a, zero), jnp.where(head_a, zero, a)], axis=0)

    def kv_body(nidx, carry):
        rows = pl.ds(pl.multiple_of(nidx * C, C), C)
        for p in range(pairs):
            ls = slice(p * LANES, (p + 1) * LANES)
            kc, vc = k_ref[0, rows, ls].astype(F32), v_ref[0, rows, ls]
            kt = jnp.concatenate([(kc * kdf_ref[p]).T, (kc * kdb_ref[p]).T], axis=0).astype(BF16)
            kv_scr[nidx, p] = _dot(kt, vc) * jnp.concatenate([same_head, same_head], axis=0)
        return carry

    def bwd_scan(it, states):
        nidx = nchunk - 1 - it
        new = []
        for p in range(pairs):
            sb_scr[nidx, p] = states[p].astype(BF16)
            new.append(states[p] * cdb_ref[p] + kv_scr[nidx, p, pl.ds(C, C), :])
        return tuple(new)

    def main_body(nidx, states):
        rows = pl.ds(pl.multiple_of(nidx * C, C), C)
        new = []
        for p in range(pairs):
            ls = slice(p * LANES, (p + 1) * LANES)
            qc, kc, vc = q_ref[0, rows, ls], k_ref[0, rows, ls], v_ref[0, rows, ls]
            s = lax.dot_general(split_heads(qc), kc, (((1,), (1,)), ((), ())), preferred_element_type=F32)
            s = (s * dm_ref[p]).astype(BF16)
            qf = (qc.astype(F32) * qdf_ref[p]).astype(BF16)
            qb = (qc.astype(F32) * qdb_ref[p]).astype(BF16)
            lhs = jnp.concatenate([s[:C], s[C:], qf, qb], axis=1)
            rhs = jnp.concatenate([split_heads(vc), states[p].astype(BF16), sb_scr[nidx, p]], axis=0)
            acc_scr[rows, ls] = _dot(lhs, rhs)
            new.append(states[p] * cdf_ref[p] + kv_scr[nidx, p, pl.ds(0, C), :])
        return tuple(new)

    init = tuple(jnp.zeros((LANES, LANES), F32) for _ in range(pairs))
    lax.fori_loop(0, nchunk, kv_body, 0, unroll=RET_UNROLL)
    lax.fori_loop(0, nchunk, bwd_scan, init)
    lax.fori_loop(0, nchunk, main_body, init, unroll=RET_UNROLL)

    tn = min(ROW_TILE, L)
    avg = avg_ref[...]

    def group_mean(a):
        hi = a.astype(BF16)
        lo = (a - hi.astype(F32)).astype(BF16)
        return _dot(hi, avg) + _dot(lo, avg)

    def norm_body(it, carry):
        rows = pl.ds(pl.multiple_of(it * tn, tn), tn)
        o = acc_scr[rows, :]
        oc = o - group_mean(o)
        var = group_mean(oc * oc)
        on = oc * lax.rsqrt(var + EPS) * gng_ref[...] + gnb_ref[...]
        g = g_ref[0, rows, :].astype(F32)
        o_ref[0, rows, :] = (on * g * (0.5 + 0.5 * jnp.tanh(0.5 * g))).astype(BF16)
        return carry

    lax.fori_loop(0, L // tn, norm_body, 0, unroll=2)


def _retention(rt, tables, gn_g, gn_b):
    B, L, _ = rt.shape
    groups = RT_W // RET_LANES
    pairs = RET_LANES // LANES
    li = jnp.arange(RET_LANES) // RT_HEAD_DIM
    avg = ((li[:, None] == li[None, :]).astype(F32) / RT_HEAD_DIM).astype(BF16)
    col = lambda which: pl.BlockSpec((1, L, RET_LANES), lambda g, b: (b, 0, which * groups + g))
    tab = lambda a: pl.BlockSpec((pairs,) + a.shape[1:], lambda g, b: (g, 0, 0))
    vec = pl.BlockSpec((1, RET_LANES), lambda g, b: (0, g))
    return pl.pallas_call(
        functools.partial(_ret_kernel, L=L),
        grid=(groups, B),
        in_specs=[col(0), col(1), col(2), col(3)] + [tab(a) for a in tables] + [_const_spec(avg.shape), vec, vec],
        out_specs=pl.BlockSpec((1, L, RET_LANES), lambda g, b: (b, 0, g)),
        out_shape=jax.ShapeDtypeStruct((B, L, RT_W), BF16),
        scratch_shapes=[pltpu.VMEM((L, RET_LANES), F32),
                        pltpu.VMEM((L // RET_CHUNK, pairs, 2 * RET_CHUNK, LANES), F32),
                        pltpu.VMEM((L // RET_CHUNK, pairs, LANES, LANES), BF16)],
        compiler_params=_params("arbitrary", "arbitrary"),
        name="retention",
    )(rt, rt, rt, rt, *tables, avg, gn_g.reshape(1, -1), gn_b.reshape(1, -1))


def _gelu_tanh(a):
    return 0.5 * a * (1.0 + jnp.tanh(math.sqrt(2.0 / math.pi) * (a + 0.044715 * (a * a * a))))


def _rms(x, g):
    return x * lax.rsqrt(jnp.mean(x * x, axis=-1, keepdims=True) + EPS) * g


def _ffn_kernel(xp_ref, xm_ref, xn_ref, hp_ref, hm_ref, hn_ref, rp_ref, rm_ref, rn_ref, hg_ref, woh_ref, wor_ref,
                g2_ref, wa_ref, wb_ref, cwa_ref, cwb_ref, cba_ref, cbb_ref, wd_ref, gf_ref, o_ref, *, tm):
    i = pl.program_id(1)
    last = pl.num_programs(1) - 1
    grp = lax.broadcasted_iota(jnp.int32, (FHALO, 1), 0)

    def cat(p, m, n):
        edge = jnp.where(grp == 0, n[0], jnp.where(grp == FHALO - 1, p[0], jnp.zeros_like(p[0])))
        return jnp.concatenate([m[0], edge], axis=0)

    yh = _rms(cat(hp_ref, hm_ref, hn_ref).astype(F32), hg_ref[...]).astype(BF16)
    x1 = cat(xp_ref, xm_ref, xn_ref) + _dot(yh, woh_ref[...]) + _dot(cat(rp_ref, rm_ref, rn_ref), wor_ref[...])
    xe = _rms(x1, g2_ref[...])
    rows_e = tm + FHALO
    row = lax.broadcasted_iota(jnp.int32, (rows_e, 1), 0)
    outside = ((row == rows_e - 1) & (i == 0)) | ((row == tm) & (i == last))
    xe = jnp.where(outside, 0.0, xe).astype(BF16)

    def conv3(u, cw_ref, cb_ref):
        return (pltpu.roll(u, 1, 0)[:tm] * cw_ref[0:1, :] + u[:tm] * cw_ref[1:2, :]
                + pltpu.roll(u, rows_e - 1, 0)[:tm] * cw_ref[2:3, :] + cb_ref[...])

    gated = (_gelu_tanh(conv3(_dot(xe, wa_ref[...]), cwa_ref, cba_ref))
             * conv3(_dot(xe, wb_ref[...]), cwb_ref, cbb_ref))
    y = x1[:tm] + _dot(gated.astype(BF16), wd_ref[...])
    o_ref[0] = _rms(y, gf_ref[...])


def _mix_ffn(x, zh, zr, hy_out_g, w_out, norm2_g, w_up, conv_w, conv_b, w_down, final_g):
    B, L, D = x.shape
    tm = min(ROW_TILE, L)
    blocks = tm // FHALO

    def halo3(w):
        prev = pl.BlockSpec((1, FHALO, w), lambda b, i: (b, jnp.maximum(i * blocks - 1, 0), 0))
        main = pl.BlockSpec((1, tm, w), lambda b, i: (b, i, 0))
        nxt = pl.BlockSpec((1, FHALO, w), lambda b, i: (b, jnp.minimum((i + 1) * blocks, L // FHALO - 1), 0))
        return [prev, main, nxt]

    consts = (hy_out_g.reshape(1, -1), w_out[:HY_W].astype(BF16), w_out[HY_W:].astype(BF16),
              norm2_g.reshape(1, D), w_up[:, :D_FF].astype(BF16), w_up[:, D_FF:].astype(BF16),
              conv_w[:, :D_FF], conv_w[:, D_FF:], conv_b[:D_FF].reshape(1, -1), conv_b[D_FF:].reshape(1, -1),
              w_down.astype(BF16), final_g.reshape(1, D))
    return pl.pallas_call(
        functools.partial(_ffn_kernel, tm=tm),
        grid=(B, L // tm),
        in_specs=halo3(D) + halo3(HY_W) + halo3(RT_W) + [_const_spec(a.shape) for a in consts],
        out_specs=pl.BlockSpec((1, tm, D), lambda b, i: (b, i, 0)),
        out_shape=jax.ShapeDtypeStruct((B, L, D), F32),
        compiler_params=_params("parallel", "arbitrary"),
        name="mix_ffn",
    )(x, x, x, zh, zh, zh, zr, zr, zr, *consts)


def _trunk(x, prm, spectra, fwd, inv, rot, ret_tables, n):
    hy, rt = _in_proj(x, prm["norm1_g"], prm["w_in"], prm["hy_conv_w"], prm["hy_conv_b"], *rot)
    z = _long_conv(hy, 0, hy, 1, spectra, 0, prm["hy_bias"][0], fwd, inv, n)
    z = _long_conv(z, 0, hy, 2, spectra, 1, prm["hy_bias"][1], fwd, inv, n)
    zr = _retention(rt, ret_tables, prm["ret_gn_g"], prm["ret_gn_b"])
    return _mix_ffn(x, z, zr, prm["hy_out_g"], prm["w_out"], prm["norm2_g"], prm["w_up"], prm["ffn_conv_w"],
                    prm["ffn_conv_b"], prm["w_down"], prm["final_g"])


def _layer(xs, prm, n=CONV_BLOCKS):
    L = xs[0].shape[1]
    assert all(x.shape[1] == L for x in xs) and L % (n * CMAC_ROWS) == 0 and L % (RET_CHUNK * RET_UNROLL) == 0
    fwd, inv = _dft_tables(L // n)
    spectra = _filter_spectra(L, n, fwd, prm["filt_w1"], prm["filt_b1"], prm["filt_w2"], prm["filt_b2"],
                              prm["filt_w3"], prm["filt_b3"], prm["filt_freq"], prm["filt_w4"])
    rot = _rotary_tables(L)
    ret_tables = _retention_tables()
    return tuple(_trunk(x, prm, spectra, fwd, inv, rot, ret_tables, n) for x in xs)


def kernel(x_prompt, x_sample, norm1_g, w_in, hy_conv_w, hy_conv_b, filt_w1, filt_b1, filt_w2, filt_b2,
           filt_w3, filt_b3, filt_freq, filt_w4, hy_bias, hy_out_g, ret_gn_g, ret_gn_b, w_out, norm2_g,
           w_up, ffn_conv_w, ffn_conv_b, w_down, final_g):
    layer = dict(norm1_g=norm1_g, w_in=w_in, hy_conv_w=hy_conv_w, hy_conv_b=hy_conv_b, filt_w1=filt_w1,
                 filt_b1=filt_b1, filt_w2=filt_w2, filt_b2=filt_b2, filt_w3=filt_w3, filt_b3=filt_b3,
                 filt_freq=filt_freq, filt_w4=filt_w4, hy_bias=hy_bias, hy_out_g=hy_out_g,
                 ret_gn_g=ret_gn_g, ret_gn_b=ret_gn_b, w_out=w_out, norm2_g=norm2_g, w_up=w_up,
                 ffn_conv_w=ffn_conv_w, ffn_conv_b=ffn_conv_b, w_down=w_down)
    assert all(v.shape[0] == 1 for v in layer.values())
    prm = {k: v[0] for k, v in layer.items()}
    prm["final_g"] = final_g
    return _layer((x_prompt, x_sample), prm)
```

```python
import functools
import math

import jax
import jax.numpy as jnp
from jax import lax
from jax.experimental import pallas as pl
from jax.experimental.pallas import tpu as pltpu

F32 = jnp.float32
BF16 = jnp.bfloat16

D_MODEL = 1024
HY_W = D_MODEL // 2
RT_W = D_MODEL - HY_W
HY_ORDER = 2
HY_COLS = (HY_ORDER + 1) * HY_W
RT_COLS = 4 * RT_W
FILTER_BANDS = 16
FILTER_HIDDEN = 64
DECAY_TARGET = 1e-2
FAST_DECAY_PCT = 0.3
SLOW_DECAY_PCT = 1.5
RT_HEADS = 8
RT_HEAD_DIM = RT_W // RT_HEADS
ROPE_BASE = 10000.0
DECAY_OFFSET_FWD = 5.0
DECAY_OFFSET_BWD = 5.5
D_FF = ((8 * D_MODEL // 3 + 127) // 128) * 128
EPS = 1e-6

LANES = 128
HALO = 8
FHALO = 16
DFT_SPLIT = 32
CONV_BLOCKS = 4
CONV_CT = 256
ROW_TILE = 512
IN_ROW_TILE = 1024
RET_CHUNK = 128
RET_UNROLL = 2
RET_LANES = 256
CMAC_ROWS = 32
VMEM_LIMIT = 58 * 1024 * 1024


def _params(*sem):
    return pltpu.CompilerParams(dimension_semantics=sem, vmem_limit_bytes=VMEM_LIMIT)


def _dot(a, b):
    return jnp.dot(a, b, preferred_element_type=F32)


def _dot_hi(a, b):
    return jnp.dot(a, b, preferred_element_type=F32, precision=lax.Precision.HIGHEST)


def _const_spec(shape):
    nd = len(shape)
    return pl.BlockSpec(shape, lambda *_: (0,) * nd, pipeline_mode=pl.Buffered(1))


def _dft_tables(P):
    k = jnp.arange(P, dtype=jnp.int32)[:, None]

    def cs(tt):
        ang = (((2 * k + 1) * tt) % (4 * P)).astype(F32) * (math.pi / (2 * P))
        return jnp.cos(ang), jnp.sin(ang)

    ca, sa = cs(DFT_SPLIT * jnp.arange(P // DFT_SPLIT, dtype=jnp.int32)[None, :])
    cb, sb = cs(jnp.arange(DFT_SPLIT, dtype=jnp.int32)[None, :])
    cos = (ca[:, :, None] * cb[:, None, :] - sa[:, :, None] * sb[:, None, :]).reshape(P, P)
    sin = (sa[:, :, None] * cb[:, None, :] + ca[:, :, None] * sb[:, None, :]).reshape(P, P)
    fwd = jnp.concatenate([cos, -sin], axis=0).astype(BF16)
    inv = jnp.concatenate([cos.T, -sin.T], axis=1).astype(BF16)
    return fwd, inv


def _rotary_tables(L):
    inv = ROPE_BASE ** (-jnp.arange(0, RT_HEAD_DIM, 2, dtype=F32) / RT_HEAD_DIM)
    ang = jnp.arange(L, dtype=F32)[:, None] * inv[None, :]
    cos, sin = jnp.cos(ang), jnp.sin(ang)
    reps = LANES // RT_HEAD_DIM
    cos_t = jnp.tile(jnp.concatenate([cos, cos], axis=1), (1, reps))
    sin_t = jnp.tile(jnp.concatenate([-sin, sin], axis=1), (1, reps))
    return cos_t, sin_t


def _retention_tables():
    C = RET_CHUNK
    heads = jnp.arange(RT_HEADS, dtype=F32)
    log_f = jnp.log(1.0 - 2.0 ** (-DECAY_OFFSET_FWD - heads))
    log_b = jnp.log(1.0 - 2.0 ** (-DECAY_OFFSET_BWD - heads))
    idx = jnp.arange(C, dtype=F32)
    diff = idx[:, None] - idx[None, :]
    lower = diff >= 0
    d_f = jnp.where(lower[None], jnp.exp(jnp.where(lower, diff, 0.0)[None] * log_f[:, None, None]), 0.0)
    d_b = jnp.where(~lower[None], jnp.exp(jnp.where(~lower, -diff, 0.0)[None] * log_b[:, None, None]), 0.0)
    dmask = (d_f + d_b).reshape(RT_HEADS // 2, 2 * C, C)

    def lanes(tab):
        rows = tab.shape[1]
        t = jnp.repeat(tab[:, :, None], RT_HEAD_DIM, axis=2)
        t = t.reshape(RT_HEADS // 2, 2, rows, RT_HEAD_DIM).transpose(0, 2, 1, 3)
        return t.reshape(RT_HEADS // 2, rows, LANES)

    qd_f = lanes(jnp.exp((idx + 1.0)[None, :] * log_f[:, None]))
    kd_f = lanes(jnp.exp((C - 1.0 - idx)[None, :] * log_f[:, None]))
    qd_b = lanes(jnp.exp((C - idx)[None, :] * log_b[:, None]))
    kd_b = lanes(jnp.exp(idx[None, :] * log_b[:, None]))
    cd_f = lanes(jnp.exp(C * log_f)[:, None])
    cd_b = lanes(jnp.exp(C * log_b)[:, None])
    return dmask, qd_f, kd_f, qd_b, kd_b, cd_f, cd_b


def _filter_kernel(w1t_ref, w1c_ref, w1s_ref, b1_ref, w2_ref, b2_ref, w3_ref, b3_ref, fr_ref, w4_ref, band_ref,
                   delta_ref, fwd_ref, g_ref, prev_scr, tap_scr, *, P, L, n):
    step = pl.program_id(0)
    e = step - n
    pos_l = jnp.abs(e * P + lax.broadcasted_iota(jnp.int32, (1, P), 1)).astype(F32)
    ang = (2.0 * math.pi / L) * pos_l * band_ref[...]
    freq = fr_ref[...]
    h = (w1t_ref[...] * (pos_l * (1.0 / (L - 1))) + _dot_hi(w1c_ref[...], jnp.cos(ang))
         + _dot_hi(w1s_ref[...], -jnp.sin(ang)) + b1_ref[...])
    h = jnp.sin(freq * h)
    h = jnp.sin(freq * (_dot_hi(w2_ref[...], h) + b2_ref[...]))
    h = jnp.sin(freq * (_dot_hi(w3_ref[...], h) + b3_ref[...]))
    row = lax.broadcasted_iota(jnp.int32, (P, 1), 0)
    t = jnp.abs(e * P + row).astype(F32) * (1.0 / (L - 1))
    direction = (e < 0).astype(jnp.int32)
    sgn = (1 - 2 * (row & 1)).astype(F32)
    for c in range(g_ref.shape[2] // CONV_CT):
        cs = pl.ds(c * CONV_CT, CONV_CT)
        taps = lax.dot_general(h, w4_ref[pl.ds(direction, 1), :, cs][0], (((0,), (0,)), ((), ())),
                               preferred_element_type=F32, precision=lax.Precision.HIGHEST)
        taps = (taps * jnp.exp(-t * delta_ref[:, cs])).astype(BF16)
        s1 = _dot(fwd_ref[...], taps)

        @pl.when(step > 0)
        def _():
            s2_re = prev_scr[pl.ds(0, P), cs] - tap_scr[:, cs]
            s2_im = prev_scr[pl.ds(P, P), cs]
            g_ref[0, pl.ds(0, P), cs] = (s1[:P] - sgn * s2_im).astype(BF16)
            g_ref[0, pl.ds(P, P), cs] = (s1[P:] + sgn * s2_re).astype(BF16)

        prev_scr[:, cs] = s1
        tap_scr[:, cs] = taps[0:1].astype(F32)


def _filter_spectra(L, n, fwd, w1, b1, w2, b2, w3, b3, freq, w4):
    P = L // n
    cols = HY_ORDER * HY_W
    bands = jnp.linspace(1e-4, FILTER_BANDS - 1, FILTER_BANDS, dtype=F32)
    deltas = jnp.abs(jnp.linspace(math.log(DECAY_TARGET) / FAST_DECAY_PCT,
                                  math.log(DECAY_TARGET) / SLOW_DECAY_PCT, HY_W, dtype=F32))
    delta = jnp.tile(deltas, HY_ORDER)[None, :]
    w4r = w4.reshape(FILTER_HIDDEN, 2, cols).transpose(1, 0, 2)
    col = lambda a: a.reshape(-1, 1)
    args = (col(w1[0]), w1[1:1 + FILTER_BANDS].T, w1[1 + FILTER_BANDS:].T, col(b1), w2.T, col(b2), w3.T, col(b3),
            col(freq), w4r, col(bands), delta, fwd)
    return pl.pallas_call(
        functools.partial(_filter_kernel, P=P, L=L, n=n),
        grid=(2 * n,),
        in_specs=[_const_spec(a.shape) for a in args],
        out_specs=pl.BlockSpec((1, 2 * P, cols), lambda s: (jnp.maximum(s - 1, 0), 0, 0)),
        out_shape=jax.ShapeDtypeStruct((2 * n - 1, 2 * P, cols), BF16),
        scratch_shapes=[pltpu.VMEM((2 * P, cols), F32), pltpu.VMEM((1, cols), F32)],
        compiler_params=_params("arbitrary"),
        name="hyena_filter",
    )(*args)


def _in_kernel(xp_ref, xm_ref, xn_ref, g1_ref, wh_ref, wr_ref, cw_ref, cb_ref, cos_ref, sin_ref,
               hy_ref, rt_ref, *, tm):
    i = pl.program_id(1)
    last = pl.num_programs(1) - 1
    xe = jnp.concatenate([xm_ref[0], xn_ref[0], xp_ref[0]], axis=0)
    xe = xe * lax.rsqrt(jnp.mean(xe * xe, axis=-1, keepdims=True) + EPS) * g1_ref[...]
    row = lax.broadcasted_iota(jnp.int32, (tm + 2 * HALO, 1), 0)
    outside = ((row >= tm + HALO) & (i == 0)) | ((row >= tm) & (row < tm + HALO) & (i == last))
    xe = jnp.where(outside, 0.0, xe).astype(BF16)
    pe = _dot(xe, wh_ref[...])
    hy = (pltpu.roll(pe, 1, 0)[:tm] * cw_ref[0:1, :] + pe[:tm] * cw_ref[1:2, :]
          + pltpu.roll(pe, tm + 2 * HALO - 1, 0)[:tm] * cw_ref[2:3, :] + cb_ref[...])
    hy_ref[0] = hy.astype(BF16)

    pr = _dot(xe[:tm], wr_ref[...])
    cos = cos_ref[...]
    sin = sin_ref[...]
    lane = lax.broadcasted_iota(jnp.int32, (1, LANES), 1)
    first_half = (lane & (RT_HEAD_DIM // 2)) == 0
    for j in range(2 * RT_W // LANES):
        xg = pr[:, j * LANES:(j + 1) * LANES]
        partner = jnp.where(first_half, pltpu.roll(xg, LANES - RT_HEAD_DIM // 2, 1),
                            pltpu.roll(xg, RT_HEAD_DIM // 2, 1))
        r = xg * cos + partner * sin
        if j >= RT_W // LANES:
            r = r * RT_HEAD_DIM ** -0.5
        rt_ref[0, :, j * LANES:(j + 1) * LANES] = r.astype(BF16)
    rt_ref[0, :, 2 * RT_W:] = pr[:, 2 * RT_W:].astype(BF16)


def _halo_specs(tm, L, D):
    blocks = tm // HALO
    prev = pl.BlockSpec((1, HALO, D), lambda b, i: (b, jnp.maximum(i * blocks - 1, 0), 0))
    main = pl.BlockSpec((1, tm, D), lambda b, i: (b, i, 0))
    nxt = pl.BlockSpec((1, HALO, D), lambda b, i: (b, jnp.minimum((i + 1) * blocks, L // HALO - 1), 0))
    return [prev, main, nxt]


def _in_proj(x, g1, w_in, cw, cb, cos_t, sin_t):
    B, L, D = x.shape
    tm = min(IN_ROW_TILE, L)
    wh = w_in[:, :HY_COLS].astype(BF16)
    wr = w_in[:, HY_COLS:].astype(BF16)
    consts = (g1.reshape(1, D), wh, wr, cw, cb.reshape(1, -1))
    return pl.pallas_call(
        functools.partial(_in_kernel, tm=tm),
        grid=(B, L // tm),
        in_specs=_halo_specs(tm, L, D) + [_const_spec(a.shape) for a in consts]
        + [pl.BlockSpec((tm, LANES), lambda b, i: (i, 0))] * 2,
        out_specs=[pl.BlockSpec((1, tm, HY_COLS), lambda b, i: (b, i, 0)),
                   pl.BlockSpec((1, tm, RT_COLS), lambda b, i: (b, i, 0))],
        out_shape=[jax.ShapeDtypeStruct((B, L, HY_COLS), BF16),
                   jax.ShapeDtypeStruct((B, L, RT_COLS), BF16)],
        compiler_params=_params("parallel", "arbitrary"),
        name="in_proj",
    )(x, x, x, *consts, cos_t, sin_t)


def _conv_kernel(u_ref, x_ref, g_ref, bias_ref, fwd_ref, inv_ref, o_ref, spec_scr, prod_scr, *, P, n):
    for j in range(n):
        spec_scr[j] = _dot(fwd_ref[...], u_ref[0, pl.ds(j * P, P), :]).astype(BF16)
    for i in range(n):
        def rows(t, carry, i=i):
            r_re = pl.ds(pl.multiple_of(t * CMAC_ROWS, CMAC_ROWS), CMAC_ROWS)
            r_im = pl.ds(pl.multiple_of(P + t * CMAC_ROWS, CMAC_ROWS), CMAC_ROWS)
            acc_re = jnp.zeros((CMAC_ROWS, CONV_CT), BF16)
            acc_im = jnp.zeros((CMAC_ROWS, CONV_CT), BF16)
            for j in range(n):
                d = i - j + n - 1
                g_re, g_im = g_ref[d, r_re, :], g_ref[d, r_im, :]
                u_re, u_im = spec_scr[j, r_re, :], spec_scr[j, r_im, :]
                acc_re += g_re * u_re - g_im * u_im
                acc_im += g_re * u_im + g_im * u_re
            prod_scr[i, r_re, :] = acc_re
            prod_scr[i, r_im, :] = acc_im
            return carry
        lax.fori_loop(0, P // CMAC_ROWS, rows, 0)
    for i in range(n):
        y = _dot(inv_ref[...], prod_scr[i]) * (1.0 / P)
        blk = pl.ds(i * P, P)
        u = u_ref[0, blk, :].astype(F32)
        o_ref[0, blk, :] = (x_ref[0, blk, :].astype(F32) * (y + u * bias_ref[...])).astype(BF16)


def _long_conv(u, u_col, gate, gate_col, spectra, order, bias, fwd, inv, n):
    B, L = u.shape[:2]
    P = L // n
    nct = HY_W // CONV_CT
    return pl.pallas_call(
        functools.partial(_conv_kernel, P=P, n=n),
        grid=(nct, B),
        in_specs=[pl.BlockSpec((1, L, CONV_CT), lambda c, b: (b, 0, u_col * nct + c)),
                  pl.BlockSpec((1, L, CONV_CT), lambda c, b: (b, 0, gate_col * nct + c)),
                  pl.BlockSpec((2 * n - 1, 2 * P, CONV_CT), lambda c, b: (0, 0, order * nct + c),
                               pipeline_mode=pl.Buffered(1)),
                  pl.BlockSpec((1, CONV_CT), lambda c, b: (0, c)),
                  _const_spec(fwd.shape), _const_spec(inv.shape)],
        out_specs=pl.BlockSpec((1, L, CONV_CT), lambda c, b: (b, 0, c)),
        out_shape=jax.ShapeDtypeStruct((B, L, HY_W), BF16),
        scratch_shapes=[pltpu.VMEM((n, 2 * P, CONV_CT), BF16)] * 2,
        compiler_params=_params("arbitrary", "arbitrary"),
        name=f"long_conv{order}",
    )(u, gate, spectra, bias.reshape(1, -1), fwd, inv)


def _ret_kernel(q_ref, k_ref, v_ref, g_ref, dm_ref, qdf_ref, kdf_ref, qdb_ref, kdb_ref, cdf_ref, cdb_ref,
                avg_ref, gng_ref, gnb_ref, o_ref, acc_scr, kv_scr, sb_scr, *, L):
    C = RET_CHUNK
    pairs = RET_LANES // LANES
    nchunk = L // C
    lane = lax.broadcasted_iota(jnp.int32, (1, LANES), 1)
    head_a = lane < RT_HEAD_DIM
    ri = lax.broadcasted_iota(jnp.int32, (LANES, LANES), 0)
    ci = lax.broadcasted_iota(jnp.int32, (LANES, LANES), 1)
    same_head = ((ri < RT_HEAD_DIM) == (ci < RT_HEAD_DIM)).astype(F32)
    zero = jnp.zeros((), BF16)

    def split_heads(a):
        return jnp.concatenate([jnp.where(head_a, a, zero), jnp.where(head_a, zero, a)], axis=0)

    def kv_body(nidx, carry):
        rows = pl.ds(pl.multiple_of(nidx * C, C), C)
        for p in range(pairs):
            ls = slice(p * LANES, (p + 1) * LANES)
            kc, vc = k_ref[0, rows, ls].astype(F32), v_ref[0, rows, ls]
            kt = jnp.concatenate([(kc * kdf_ref[p]).T, (kc * kdb_ref[p]).T], axis=0).astype(BF16)
            kv_scr[nidx, p] = _dot(kt, vc) * jnp.concatenate([same_head, same_head], axis=0)
        return carry

    def bwd_scan(it, states):
        nidx = nchunk - 1 - it
        new = []
        for p in range(pairs):
            sb_scr[nidx, p] = states[p].astype(BF16)
            new.append(states[p] * cdb_ref[p] + kv_scr[nidx, p, pl.ds(C, C), :])
        return tuple(new)

    def main_body(nidx, states):
        rows = pl.ds(pl.multiple_of(nidx * C, C), C)
        new = []
        for p in range(pairs):
            ls = slice(p * LANES, (p + 1) * LANES)
            qc, kc, vc = q_ref[0, rows, ls], k_ref[0, rows, ls], v_ref[0, rows, ls]
            s = lax.dot_general(split_heads(qc), kc, (((1,), (1,)), ((), ())), preferred_element_type=F32)
            s = (s * dm_ref[p]).astype(BF16)
            qf = (qc.astype(F32) * qdf_ref[p]).astype(BF16)
            qb = (qc.astype(F32) * qdb_ref[p]).astype(BF16)
            lhs = jnp.concatenate([s[:C], s[C:], qf, qb], axis=1)
            rhs = jnp.concatenate([split_heads(vc), states[p].astype(BF16), sb_scr[nidx, p]], axis=0)
            acc_scr[rows, ls] = _dot(lhs, rhs)
            new.append(states[p] * cdf_ref[p] + kv_scr[nidx, p, pl.ds(0, C), :])
        return tuple(new)

    init = tuple(jnp.zeros((LANES, LANES), F32) for _ in range(pairs))
    lax.fori_loop(0, nchunk, kv_body, 0, unroll=RET_UNROLL)
    lax.fori_loop(0, nchunk, bwd_scan, init)
    lax.fori_loop(0, nchunk, main_body, init, unroll=RET_UNROLL)

    tn = min(ROW_TILE, L)
    avg = avg_ref[...]

    def group_mean(a):
        hi = a.astype(BF16)
        lo = (a - hi.astype(F32)).astype(BF16)
        return _dot(hi, avg) + _dot(lo, avg)

    def norm_body(it, carry):
        rows = pl.ds(pl.multiple_of(it * tn, tn), tn)
        o = acc_scr[rows, :]
        oc = o - group_mean(o)
        var = group_mean(oc * oc)
        on = oc * lax.rsqrt(var + EPS) * gng_ref[...] + gnb_ref[...]
        g = g_ref[0, rows, :].astype(F32)
        o_ref[0, rows, :] = (on * g * (0.5 + 0.5 * jnp.tanh(0.5 * g))).astype(BF16)
        return carry

    lax.fori_loop(0, L // tn, norm_body, 0, unroll=2)


def _retention(rt, tables, gn_g, gn_b):
    B, L, _ = rt.shape
    groups = RT_W // RET_LANES
    pairs = RET_LANES // LANES
    li = jnp.arange(RET_LANES) // RT_HEAD_DIM
    avg = ((li[:, None] == li[None, :]).astype(F32) / RT_HEAD_DIM).astype(BF16)
    col = lambda which: pl.BlockSpec((1, L, RET_LANES), lambda g, b: (b, 0, which * groups + g))
    tab = lambda a: pl.BlockSpec((pairs,) + a.shape[1:], lambda g, b: (g, 0, 0))
    vec = pl.BlockSpec((1, RET_LANES), lambda g, b: (0, g))
    return pl.pallas_call(
        functools.partial(_ret_kernel, L=L),
        grid=(groups, B),
        in_specs=[col(0), col(1), col(2), col(3)] + [tab(a) for a in tables] + [_const_spec(avg.shape), vec, vec],
        out_specs=pl.BlockSpec((1, L, RET_LANES), lambda g, b: (b, 0, g)),
        out_shape=jax.ShapeDtypeStruct((B, L, RT_W), BF16),
        scratch_shapes=[pltpu.VMEM((L, RET_LANES), F32),
                        pltpu.VMEM((L // RET_CHUNK, pairs, 2 * RET_CHUNK, LANES), F32),
                        pltpu.VMEM((L // RET_CHUNK, pairs, LANES, LANES), BF16)],
        compiler_params=_params("arbitrary", "arbitrary"),
        name="retention",
    )(rt, rt, rt, rt, *tables, avg, gn_g.reshape(1, -1), gn_b.reshape(1, -1))


def _gelu_tanh(a):
    return 0.5 * a * (1.0 + jnp.tanh(math.sqrt(2.0 / math.pi) * (a + 0.044715 * (a * a * a))))


def _rms(x, g):
    return x * lax.rsqrt(jnp.mean(x * x, axis=-1, keepdims=True) + EPS) * g


def _ffn_kernel(xp_ref, xm_ref, xn_ref, hp_ref, hm_ref, hn_ref, rp_ref, rm_ref, rn_ref, hg_ref, woh_ref, wor_ref,
                g2_ref, wa_ref, wb_ref, cwa_ref, cwb_ref, cba_ref, cbb_ref, wd_ref, gf_ref, o_ref,
                ua_scr, ub_scr, *, tm):
    i = pl.program_id(1)
    last = pl.num_programs(1) - 1
    cat = lambda p, m, n: jnp.concatenate([p[0], m[0], n[0]], axis=0)
    yh = _rms(cat(hp_ref, hm_ref, hn_ref).astype(F32), hg_ref[...]).astype(BF16)
    x1 = cat(xp_ref, xm_ref, xn_ref) + _dot(yh, woh_ref[...]) + _dot(cat(rp_ref, rm_ref, rn_ref), wor_ref[...])
    xe = _rms(x1, g2_ref[...])
    row = lax.broadcasted_iota(jnp.int32, (tm + 2 * FHALO, 1), 0)
    outside = ((row < FHALO) & (i == 0)) | ((row >= tm + FHALO) & (i == last))
    xe = jnp.where(outside, 0.0, xe).astype(BF16)

    def conv3(scr, cw_ref, cb_ref):
        return (scr[pl.ds(FHALO - 1, tm), :] * cw_ref[0:1, :] + scr[pl.ds(FHALO, tm), :] * cw_ref[1:2, :]
                + scr[pl.ds(FHALO + 1, tm), :] * cw_ref[2:3, :] + cb_ref[...])

    ua_scr[...] = _dot(xe, wa_ref[...])
    ub_scr[...] = _dot(xe, wb_ref[...])
    gated = _gelu_tanh(conv3(ua_scr, cwa_ref, cba_ref)) * conv3(ub_scr, cwb_ref, cbb_ref)
    y = x1[FHALO:FHALO + tm] + _dot(gated.astype(BF16), wd_ref[...])
    o_ref[0] = _rms(y, gf_ref[...])


def _mix_ffn(x, zh, zr, hy_out_g, w_out, norm2_g, w_up, conv_w, conv_b, w_down, final_g):
    B, L, D = x.shape
    tm = min(ROW_TILE, L)
    blocks = tm // FHALO

    def halo3(w):
        prev = pl.BlockSpec((1, FHALO, w), lambda b, i: (b, jnp.maximum(i * blocks - 1, 0), 0))
        main = pl.BlockSpec((1, tm, w), lambda b, i: (b, i, 0))
        nxt = pl.BlockSpec((1, FHALO, w), lambda b, i: (b, jnp.minimum((i + 1) * blocks, L // FHALO - 1), 0))
        return [prev, main, nxt]

    consts = (hy_out_g.reshape(1, -1), w_out[:HY_W].astype(BF16), w_out[HY_W:].astype(BF16),
              norm2_g.reshape(1, D), w_up[:, :D_FF].astype(BF16), w_up[:, D_FF:].astype(BF16),
              conv_w[:, :D_FF], conv_w[:, D_FF:], conv_b[:D_FF].reshape(1, -1), conv_b[D_FF:].reshape(1, -1),
              w_down.astype(BF16), final_g.reshape(1, D))
    return pl.pallas_call(
        functools.partial(_ffn_kernel, tm=tm),
        grid=(B, L // tm),
        in_specs=halo3(D) + halo3(HY_W) + halo3(RT_W) + [_const_spec(a.shape) for a in consts],
        out_specs=pl.BlockSpec((1, tm, D), lambda b, i: (b, i, 0)),
        out_shape=jax.ShapeDtypeStruct((B, L, D), F32),
        scratch_shapes=[pltpu.VMEM((tm + 2 * FHALO, D_FF), F32)] * 2,
        compiler_params=_params("parallel", "arbitrary"),
        name="mix_ffn",
    )(x, x, x, zh, zh, zh, zr, zr, zr, *consts)


def _trunk(x, prm, spectra, fwd, inv, rot, ret_tables, n):
    hy, rt = _in_proj(x, prm["norm1_g"], prm["w_in"], prm["hy_conv_w"], prm["hy_conv_b"], *rot)
    z = _long_conv(hy, 0, hy, 1, spectra, 0, prm["hy_bias"][0], fwd, inv, n)
    z = _long_conv(z, 0, hy, 2, spectra, 1, prm["hy_bias"][1], fwd, inv, n)
    zr = _retention(rt, ret_tables, prm["ret_gn_g"], prm["ret_gn_b"])
    return _mix_ffn(x, z, zr, prm["hy_out_g"], prm["w_out"], prm["norm2_g"], prm["w_up"], prm["ffn_conv_w"],
                    prm["ffn_conv_b"], prm["w_down"], prm["final_g"])


def _layer(xs, prm, n=CONV_BLOCKS):
    L = xs[0].shape[1]
    assert all(x.shape[1] == L for x in xs) and L % (n * CMAC_ROWS) == 0 and L % (RET_CHUNK * RET_UNROLL) == 0
    fwd, inv = _dft_tables(L // n)
    spectra = _filter_spectra(L, n, fwd, prm["filt_w1"], prm["filt_b1"], prm["filt_w2"], prm["filt_b2"],
                              prm["filt_w3"], prm["filt_b3"], prm["filt_freq"], prm["filt_w4"])
    rot = _rotary_tables(L)
    ret_tables = _retention_tables()
    return tuple(_trunk(x, prm, spectra, fwd, inv, rot, ret_tables, n) for x in xs)


def kernel(x_prompt, x_sample, norm1_g, w_in, hy_conv_w, hy_conv_b, filt_w1, filt_b1, filt_w2, filt_b2,
           filt_w3, filt_b3, filt_freq, filt_w4, hy_bias, hy_out_g, ret_gn_g, ret_gn_b, w_out, norm2_g,
           w_up, ffn_conv_w, ffn_conv_b, w_down, final_g):
    layer = dict(norm1_g=norm1_g, w_in=w_in, hy_conv_w=hy_conv_w, hy_conv_b=hy_conv_b, filt_w1=filt_w1,
                 filt_b1=filt_b1, filt_w2=filt_w2, filt_b2=filt_b2, filt_w3=filt_w3, filt_b3=filt_b3,
                 filt_freq=filt_freq, filt_w4=filt_w4, hy_bias=hy_bias, hy_out_g=hy_out_g,
                 ret_gn_g=ret_gn_g, ret_gn_b=ret_gn_b, w_out=w_out, norm2_g=norm2_g, w_up=w_up,
                 ffn_conv_w=ffn_conv_w, ffn_conv_b=ffn_conv_b, w_down=w_down)
    assert all(v.shape[0] == 1 for v in layer.values())
    prm = {k: v[0] for k, v in layer.items()}
    prm["final_g"] = final_g
    return _layer((x_prompt, x_sample), prm)
```

```python
import functools
import math

import jax
import jax.numpy as jnp
from jax import lax
from jax.experimental import pallas as pl
from jax.experimental.pallas import tpu as pltpu

F32 = jnp.float32
BF16 = jnp.bfloat16

D_MODEL = 1024
HY_W = D_MODEL // 2
RT_W = D_MODEL - HY_W
HY_ORDER = 2
HY_COLS = (HY_ORDER + 1) * HY_W
RT_COLS = 4 * RT_W
FILTER_BANDS = 16
FILTER_HIDDEN = 64
DECAY_TARGET = 1e-2
FAST_DECAY_PCT = 0.3
SLOW_DECAY_PCT = 1.5
RT_HEADS = 8
RT_HEAD_DIM = RT_W // RT_HEADS
ROPE_BASE = 10000.0
DECAY_OFFSET_FWD = 5.0
DECAY_OFFSET_BWD = 5.5
D_FF = ((8 * D_MODEL // 3 + 127) // 128) * 128
EPS = 1e-6

LANES = 128
HALO = 8
FHALO = 16
DFT_SPLIT = 32
CONV_BLOCKS = 4
CONV_CT = 256
ROW_TILE = 512
IN_ROW_TILE = 1024
RET_CHUNK = 256
RET_UNROLL = 2
RET_LANES = 256
CMAC_ROWS = 128
VMEM_LIMIT = 58 * 1024 * 1024


def _params(*sem):
    return pltpu.CompilerParams(dimension_semantics=sem, vmem_limit_bytes=VMEM_LIMIT)


def _dot(a, b):
    return jnp.dot(a, b, preferred_element_type=F32)


def _dot_hi(a, b):
    return jnp.dot(a, b, preferred_element_type=F32, precision=lax.Precision.HIGHEST)


def _const_spec(shape):
    nd = len(shape)
    return pl.BlockSpec(shape, lambda *_: (0,) * nd, pipeline_mode=pl.Buffered(1))


def _dft_tables(P):
    k = jnp.arange(P, dtype=jnp.int32)[:, None]

    def cs(tt):
        ang = (((2 * k + 1) * tt) % (4 * P)).astype(F32) * (math.pi / (2 * P))
        return jnp.cos(ang), jnp.sin(ang)

    ca, sa = cs(DFT_SPLIT * jnp.arange(P // DFT_SPLIT, dtype=jnp.int32)[None, :])
    cb, sb = cs(jnp.arange(DFT_SPLIT, dtype=jnp.int32)[None, :])
    cos = (ca[:, :, None] * cb[:, None, :] - sa[:, :, None] * sb[:, None, :]).reshape(P, P)
    sin = (sa[:, :, None] * cb[:, None, :] + ca[:, :, None] * sb[:, None, :]).reshape(P, P)
    fwd = jnp.concatenate([cos, -sin], axis=0).astype(BF16)
    inv = jnp.concatenate([cos.T, -sin.T], axis=1).astype(BF16)
    return fwd, inv


def _rotary_tables(L):
    inv = ROPE_BASE ** (-jnp.arange(0, RT_HEAD_DIM, 2, dtype=F32) / RT_HEAD_DIM)
    ang = jnp.arange(L, dtype=F32)[:, None] * inv[None, :]
    cos, sin = jnp.cos(ang), jnp.sin(ang)
    reps = LANES // RT_HEAD_DIM
    cos_t = jnp.tile(jnp.concatenate([cos, cos], axis=1), (1, reps))
    sin_t = jnp.tile(jnp.concatenate([-sin, sin], axis=1), (1, reps))
    return cos_t, sin_t


def _retention_tables():
    C = RET_CHUNK
    heads = jnp.arange(RT_HEADS, dtype=F32)
    log_f = jnp.log(1.0 - 2.0 ** (-DECAY_OFFSET_FWD - heads))
    log_b = jnp.log(1.0 - 2.0 ** (-DECAY_OFFSET_BWD - heads))
    idx = jnp.arange(C, dtype=F32)
    diff = idx[:, None] - idx[None, :]
    lower = diff >= 0
    d_f = jnp.where(lower[None], jnp.exp(jnp.where(lower, diff, 0.0)[None] * log_f[:, None, None]), 0.0)
    d_b = jnp.where(~lower[None], jnp.exp(jnp.where(~lower, -diff, 0.0)[None] * log_b[:, None, None]), 0.0)
    dmask = (d_f + d_b).reshape(RT_HEADS // 2, 2 * C, C)

    def lanes(tab):
        rows = tab.shape[1]
        t = jnp.repeat(tab[:, :, None], RT_HEAD_DIM, axis=2)
        t = t.reshape(RT_HEADS // 2, 2, rows, RT_HEAD_DIM).transpose(0, 2, 1, 3)
        return t.reshape(RT_HEADS // 2, rows, LANES)

    qd_f = lanes(jnp.exp((idx + 1.0)[None, :] * log_f[:, None]))
    kd_f = lanes(jnp.exp((C - 1.0 - idx)[None, :] * log_f[:, None]))
    qd_b = lanes(jnp.exp((C - idx)[None, :] * log_b[:, None]))
    kd_b = lanes(jnp.exp(idx[None, :] * log_b[:, None]))
    cd_f = lanes(jnp.exp(C * log_f)[:, None])
    cd_b = lanes(jnp.exp(C * log_b)[:, None])
    return dmask, qd_f, kd_f, qd_b, kd_b, cd_f, cd_b


def _filter_kernel(w1t_ref, w1c_ref, w1s_ref, b1_ref, w2_ref, b2_ref, w3_ref, b3_ref, fr_ref, w4_ref, band_ref,
                   delta_ref, fwd_ref, g_ref, prev_scr, tap_scr, *, P, L, n):
    step = pl.program_id(0)
    e = step - n
    pos_l = jnp.abs(e * P + lax.broadcasted_iota(jnp.int32, (1, P), 1)).astype(F32)
    ang = (2.0 * math.pi / L) * pos_l * band_ref[...]
    freq = fr_ref[...]
    h = (w1t_ref[...] * (pos_l * (1.0 / (L - 1))) + _dot_hi(w1c_ref[...], jnp.cos(ang))
         + _dot_hi(w1s_ref[...], -jnp.sin(ang)) + b1_ref[...])
    h = jnp.sin(freq * h)
    h = jnp.sin(freq * (_dot_hi(w2_ref[...], h) + b2_ref[...]))
    h = jnp.sin(freq * (_dot_hi(w3_ref[...], h) + b3_ref[...]))
    row = lax.broadcasted_iota(jnp.int32, (P, 1), 0)
    t = jnp.abs(e * P + row).astype(F32) * (1.0 / (L - 1))
    direction = (e < 0).astype(jnp.int32)
    sgn = (1 - 2 * (row & 1)).astype(F32)
    for c in range(g_ref.shape[2] // CONV_CT):
        cs = pl.ds(c * CONV_CT, CONV_CT)
        taps = lax.dot_general(h, w4_ref[pl.ds(direction, 1), :, cs][0], (((0,), (0,)), ((), ())),
                               preferred_element_type=F32, precision=lax.Precision.HIGHEST)
        taps = (taps * jnp.exp(-t * delta_ref[:, cs])).astype(BF16)
        s1 = _dot(fwd_ref[...], taps)

        @pl.when(step > 0)
        def _():
            s2_re = prev_scr[pl.ds(0, P), cs] - tap_scr[:, cs]
            s2_im = prev_scr[pl.ds(P, P), cs]
            g_ref[0, pl.ds(0, P), cs] = (s1[:P] - sgn * s2_im).astype(BF16)
            g_ref[0, pl.ds(P, P), cs] = (s1[P:] + sgn * s2_re).astype(BF16)

        prev_scr[:, cs] = s1
        tap_scr[:, cs] = taps[0:1].astype(F32)


def _filter_spectra(L, n, fwd, w1, b1, w2, b2, w3, b3, freq, w4):
    P = L // n
    cols = HY_ORDER * HY_W
    bands = jnp.linspace(1e-4, FILTER_BANDS - 1, FILTER_BANDS, dtype=F32)
    deltas = jnp.abs(jnp.linspace(math.log(DECAY_TARGET) / FAST_DECAY_PCT,
                                  math.log(DECAY_TARGET) / SLOW_DECAY_PCT, HY_W, dtype=F32))
    delta = jnp.tile(deltas, HY_ORDER)[None, :]
    w4r = w4.reshape(FILTER_HIDDEN, 2, cols).transpose(1, 0, 2)
    col = lambda a: a.reshape(-1, 1)
    args = (col(w1[0]), w1[1:1 + FILTER_BANDS].T, w1[1 + FILTER_BANDS:].T, col(b1), w2.T, col(b2), w3.T, col(b3),
            col(freq), w4r, col(bands), delta, fwd)
    return pl.pallas_call(
        functools.partial(_filter_kernel, P=P, L=L, n=n),
        grid=(2 * n,),
        in_specs=[_const_spec(a.shape) for a in args],
        out_specs=pl.BlockSpec((1, 2 * P, cols), lambda s: (jnp.maximum(s - 1, 0), 0, 0)),
        out_shape=jax.ShapeDtypeStruct((2 * n - 1, 2 * P, cols), BF16),
        scratch_shapes=[pltpu.VMEM((2 * P, cols), F32), pltpu.VMEM((1, cols), F32)],
        compiler_params=_params("arbitrary"),
        name="hyena_filter",
    )(*args)


def _in_kernel(xp_ref, xm_ref, xn_ref, g1_ref, wh_ref, wr_ref, cw_ref, cb_ref, cos_ref, sin_ref,
               hy_ref, rt_ref, *, tm):
    i = pl.program_id(1)
    last = pl.num_programs(1) - 1
    xe = jnp.concatenate([xm_ref[0], xn_ref[0], xp_ref[0]], axis=0)
    xe = xe * lax.rsqrt(jnp.mean(xe * xe, axis=-1, keepdims=True) + EPS) * g1_ref[...]
    row = lax.broadcasted_iota(jnp.int32, (tm + 2 * HALO, 1), 0)
    outside = ((row >= tm + HALO) & (i == 0)) | ((row >= tm) & (row < tm + HALO) & (i == last))
    xe = jnp.where(outside, 0.0, xe).astype(BF16)
    pe = _dot(xe, wh_ref[...])
    hy = (pltpu.roll(pe, 1, 0)[:tm] * cw_ref[0:1, :] + pe[:tm] * cw_ref[1:2, :]
          + pltpu.roll(pe, tm + 2 * HALO - 1, 0)[:tm] * cw_ref[2:3, :] + cb_ref[...])
    hy_ref[0] = hy.astype(BF16)

    pr = _dot(xe[:tm], wr_ref[...])
    cos = cos_ref[...]
    sin = sin_ref[...]
    lane = lax.broadcasted_iota(jnp.int32, (1, LANES), 1)
    first_half = (lane & (RT_HEAD_DIM // 2)) == 0
    for j in range(2 * RT_W // LANES):
        xg = pr[:, j * LANES:(j + 1) * LANES]
        partner = jnp.where(first_half, pltpu.roll(xg, LANES - RT_HEAD_DIM // 2, 1),
                            pltpu.roll(xg, RT_HEAD_DIM // 2, 1))
        r = xg * cos + partner * sin
        if j >= RT_W // LANES:
            r = r * RT_HEAD_DIM ** -0.5
        rt_ref[0, :, j * LANES:(j + 1) * LANES] = r.astype(BF16)
    rt_ref[0, :, 2 * RT_W:] = pr[:, 2 * RT_W:].astype(BF16)


def _halo_specs(tm, L, D):
    blocks = tm // HALO
    prev = pl.BlockSpec((1, HALO, D), lambda b, i: (b, jnp.maximum(i * blocks - 1, 0), 0))
    main = pl.BlockSpec((1, tm, D), lambda b, i: (b, i, 0))
    nxt = pl.BlockSpec((1, HALO, D), lambda b, i: (b, jnp.minimum((i + 1) * blocks, L // HALO - 1), 0))
    return [prev, main, nxt]


def _in_proj(x, g1, w_in, cw, cb, cos_t, sin_t):
    B, L, D = x.shape
    tm = min(IN_ROW_TILE, L)
    wh = w_in[:, :HY_COLS].astype(BF16)
    wr = w_in[:, HY_COLS:].astype(BF16)
    consts = (g1.reshape(1, D), wh, wr, cw, cb.reshape(1, -1))
    return pl.pallas_call(
        functools.partial(_in_kernel, tm=tm),
        grid=(B, L // tm),
        in_specs=_halo_specs(tm, L, D) + [_const_spec(a.shape) for a in consts]
        + [pl.BlockSpec((tm, LANES), lambda b, i: (i, 0))] * 2,
        out_specs=[pl.BlockSpec((1, tm, HY_COLS), lambda b, i: (b, i, 0)),
                   pl.BlockSpec((1, tm, RT_COLS), lambda b, i: (b, i, 0))],
        out_shape=[jax.ShapeDtypeStruct((B, L, HY_COLS), BF16),
                   jax.ShapeDtypeStruct((B, L, RT_COLS), BF16)],
        compiler_params=_params("parallel", "arbitrary"),
        name="in_proj",
    )(x, x, x, *consts, cos_t, sin_t)


def _conv_kernel(u_ref, x_ref, g_ref, bias_ref, fwd_ref, inv_ref, o_ref, spec_scr, prod_scr, *, P, n):
    for j in range(n):
        spec_scr[j] = _dot(fwd_ref[...], u_ref[0, pl.ds(j * P, P), :]).astype(BF16)
    for i in range(n):
        def rows(t, carry, i=i):
            r_re = pl.ds(pl.multiple_of(t * CMAC_ROWS, CMAC_ROWS), CMAC_ROWS)
            r_im = pl.ds(pl.multiple_of(P + t * CMAC_ROWS, CMAC_ROWS), CMAC_ROWS)
            acc_re = jnp.zeros((CMAC_ROWS, CONV_CT), BF16)
            acc_im = jnp.zeros((CMAC_ROWS, CONV_CT), BF16)
            for j in range(n):
                d = i - j + n - 1
                g_re, g_im = g_ref[d, r_re, :], g_ref[d, r_im, :]
                u_re, u_im = spec_scr[j, r_re, :], spec_scr[j, r_im, :]
                acc_re += g_re * u_re - g_im * u_im
                acc_im += g_re * u_im + g_im * u_re
            prod_scr[i, r_re, :] = acc_re
            prod_scr[i, r_im, :] = acc_im
            return carry
        lax.fori_loop(0, P // CMAC_ROWS, rows, 0)
    for i in range(n):
        y = _dot(inv_ref[...], prod_scr[i]) * (1.0 / P)
        blk = pl.ds(i * P, P)
        u = u_ref[0, blk, :].astype(F32)
        o_ref[0, blk, :] = (x_ref[0, blk, :].astype(F32) * (y + u * bias_ref[...])).astype(BF16)


def _long_conv(u, u_col, gate, gate_col, spectra, order, bias, fwd, inv, n):
    B, L = u.shape[:2]
    P = L // n
    nct = HY_W // CONV_CT
    return pl.pallas_call(
        functools.partial(_conv_kernel, P=P, n=n),
        grid=(nct, B),
        in_specs=[pl.BlockSpec((1, L, CONV_CT), lambda c, b: (b, 0, u_col * nct + c)),
                  pl.BlockSpec((1, L, CONV_CT), lambda c, b: (b, 0, gate_col * nct + c)),
                  pl.BlockSpec((2 * n - 1, 2 * P, CONV_CT), lambda c, b: (0, 0, order * nct + c),
                               pipeline_mode=pl.Buffered(1)),
                  pl.BlockSpec((1, CONV_CT), lambda c, b: (0, c)),
                  _const_spec(fwd.shape), _const_spec(inv.shape)],
        out_specs=pl.BlockSpec((1, L, CONV_CT), lambda c, b: (b, 0, c)),
        out_shape=jax.ShapeDtypeStruct((B, L, HY_W), BF16),
        scratch_shapes=[pltpu.VMEM((n, 2 * P, CONV_CT), BF16)] * 2,
        compiler_params=_params("arbitrary", "arbitrary"),
        name=f"long_conv{order}",
    )(u, gate, spectra, bias.reshape(1, -1), fwd, inv)


def _ret_kernel(q_ref, k_ref, v_ref, g_ref, dm_ref, qdf_ref, kdf_ref, qdb_ref, kdb_ref, cdf_ref, cdb_ref,
                avg_ref, gng_ref, gnb_ref, o_ref, acc_scr, kv_scr, sb_scr, *, L):
    C = RET_CHUNK
    pairs = RET_LANES // LANES
    nchunk = L // C
    lane = lax.broadcasted_iota(jnp.int32, (1, LANES), 1)
    head_a = lane < RT_HEAD_DIM
    ri = lax.broadcasted_iota(jnp.int32, (LANES, LANES), 0)
    ci = lax.broadcasted_iota(jnp.int32, (LANES, LANES), 1)
    same_head = ((ri < RT_HEAD_DIM) == (ci < RT_HEAD_DIM)).astype(F32)
    zero = jnp.zeros((), BF16)

    def split_heads(a):
        return jnp.concatenate([jnp.where(head_a, a, zero), jnp.where(head_a, zero, a)], axis=0)

    def kv_body(nidx, carry):
        rows = pl.ds(pl.multiple_of(nidx * C, C), C)
        for p in range(pairs):
            ls = slice(p * LANES, (p + 1) * LANES)
            kc, vc = k_ref[0, rows, ls].astype(F32), v_ref[0, rows, ls]
            kt = jnp.concatenate([(kc * kdf_ref[p]).T, (kc * kdb_ref[p]).T], axis=0).astype(BF16)
            kv_scr[nidx, p] = _dot(kt, vc) * jnp.concatenate([same_head, same_head], axis=0)
        return carry

    def bwd_scan(it, states):
        nidx = nchunk - 1 - it
        new = []
        for p in range(pairs):
            sb_scr[nidx, p] = states[p].astype(BF16)
            new.append(states[p] * cdb_ref[p] + kv_scr[nidx, p, pl.ds(LANES, LANES), :])
        return tuple(new)

    def main_body(nidx, states):
        rows = pl.ds(pl.multiple_of(nidx * C, C), C)
        new = []
        for p in range(pairs):
            ls = slice(p * LANES, (p + 1) * LANES)
            qc, kc, vc = q_ref[0, rows, ls], k_ref[0, rows, ls], v_ref[0, rows, ls]
            s = lax.dot_general(split_heads(qc), kc, (((1,), (1,)), ((), ())), preferred_element_type=F32)
            s = (s * dm_ref[p]).astype(BF16)
            qf = (qc.astype(F32) * qdf_ref[p]).astype(BF16)
            qb = (qc.astype(F32) * qdb_ref[p]).astype(BF16)
            lhs = jnp.concatenate([s[:C], s[C:], qf, qb], axis=1)
            rhs = jnp.concatenate([split_heads(vc), states[p].astype(BF16), sb_scr[nidx, p]], axis=0)
            acc_scr[rows, ls] = _dot(lhs, rhs)
            new.append(states[p] * cdf_ref[p] + kv_scr[nidx, p, pl.ds(0, LANES), :])
        return tuple(new)

    init = tuple(jnp.zeros((LANES, LANES), F32) for _ in range(pairs))
    lax.fori_loop(0, nchunk, kv_body, 0, unroll=RET_UNROLL)
    lax.fori_loop(0, nchunk, bwd_scan, init)
    lax.fori_loop(0, nchunk, main_body, init, unroll=RET_UNROLL)

    tn = min(ROW_TILE, L)
    avg = avg_ref[...]

    def group_mean(a):
        hi = a.astype(BF16)
        lo = (a - hi.astype(F32)).astype(BF16)
        return _dot(hi, avg) + _dot(lo, avg)

    def norm_body(it, carry):
        rows = pl.ds(pl.multiple_of(it * tn, tn), tn)
        o = acc_scr[rows, :]
        oc = o - group_mean(o)
        var = _dot((oc * oc).astype(BF16), avg)
        on = oc * lax.rsqrt(var + EPS) * gng_ref[...] + gnb_ref[...]
        g = g_ref[0, rows, :].astype(F32)
        o_ref[0, rows, :] = (on * g * (0.5 + 0.5 * jnp.tanh(0.5 * g))).astype(BF16)
        return carry

    lax.fori_loop(0, L // tn, norm_body, 0, unroll=2)


def _retention(rt, tables, gn_g, gn_b):
    B, L, _ = rt.shape
    groups = RT_W // RET_LANES
    pairs = RET_LANES // LANES
    li = jnp.arange(RET_LANES) // RT_HEAD_DIM
    avg = ((li[:, None] == li[None, :]).astype(F32) / RT_HEAD_DIM).astype(BF16)
    col = lambda which: pl.BlockSpec((1, L, RET_LANES), lambda g, b: (b, 0, which * groups + g))
    tab = lambda a: pl.BlockSpec((pairs,) + a.shape[1:], lambda g, b: (g, 0, 0))
    vec = pl.BlockSpec((1, RET_LANES), lambda g, b: (0, g))
    return pl.pallas_call(
        functools.partial(_ret_kernel, L=L),
        grid=(groups, B),
        in_specs=[col(0), col(1), col(2), col(3)] + [tab(a) for a in tables] + [_const_spec(avg.shape), vec, vec],
        out_specs=pl.BlockSpec((1, L, RET_LANES), lambda g, b: (b, 0, g)),
        out_shape=jax.ShapeDtypeStruct((B, L, RT_W), BF16),
        scratch_shapes=[pltpu.VMEM((L, RET_LANES), F32),
                        pltpu.VMEM((L // RET_CHUNK, pairs, 2 * LANES, LANES), F32),
                        pltpu.VMEM((L // RET_CHUNK, pairs, LANES, LANES), BF16)],
        compiler_params=_params("arbitrary", "arbitrary"),
        name="retention",
    )(rt, rt, rt, rt, *tables, avg, gn_g.reshape(1, -1), gn_b.reshape(1, -1))


def _gelu_tanh(a):
    return 0.5 * a * (1.0 + jnp.tanh(math.sqrt(2.0 / math.pi) * (a + 0.044715 * (a * a * a))))


def _rms(x, g):
    return x * lax.rsqrt(jnp.mean(x * x, axis=-1, keepdims=True) + EPS) * g


def _ffn_kernel(xp_ref, xm_ref, xn_ref, hp_ref, hm_ref, hn_ref, rp_ref, rm_ref, rn_ref, hg_ref, woh_ref, wor_ref,
                g2_ref, wa_ref, wb_ref, cwa_ref, cwb_ref, cba_ref, cbb_ref, wd_ref, gf_ref, o_ref,
                ua_scr, ub_scr, *, tm):
    i = pl.program_id(1)
    last = pl.num_programs(1) - 1
    def cat(p, m, n):
        return jnp.concatenate([p[0].astype(F32)[FHALO - HALO:], m[0].astype(F32), n[0].astype(F32)[:HALO]], axis=0)

    yh = _rms(cat(hp_ref, hm_ref, hn_ref), hg_ref[...]).astype(BF16)
    x1 = (cat(xp_ref, xm_ref, xn_ref) + _dot(yh, woh_ref[...])
          + _dot(cat(rp_ref, rm_ref, rn_ref).astype(BF16), wor_ref[...]))
    xe = _rms(x1, g2_ref[...])
    row = lax.broadcasted_iota(jnp.int32, (tm + 2 * HALO, 1), 0)
    outside = ((row < HALO) & (i == 0)) | ((row >= tm + HALO) & (i == last))
    xe = jnp.where(outside, 0.0, xe).astype(BF16)

    def conv3(scr, cw_ref, cb_ref):
        return (scr[pl.ds(HALO - 1, tm), :] * cw_ref[0:1, :] + scr[pl.ds(HALO, tm), :] * cw_ref[1:2, :]
                + scr[pl.ds(HALO + 1, tm), :] * cw_ref[2:3, :] + cb_ref[...])

    ua_scr[...] = _dot(xe, wa_ref[...])
    ub_scr[...] = _dot(xe, wb_ref[...])
    gated = _gelu_tanh(conv3(ua_scr, cwa_ref, cba_ref)) * conv3(ub_scr, cwb_ref, cbb_ref)
    y = x1[HALO:HALO + tm] + _dot(gated.astype(BF16), wd_ref[...])
    o_ref[0] = _rms(y, gf_ref[...])


def _mix_ffn(x, zh, zr, hy_out_g, w_out, norm2_g, w_up, conv_w, conv_b, w_down, final_g):
    B, L, D = x.shape
    tm = min(ROW_TILE, L)
    blocks = tm // FHALO

    def halo3(w):
        prev = pl.BlockSpec((1, FHALO, w), lambda b, i: (b, jnp.maximum(i * blocks - 1, 0), 0))
        main = pl.BlockSpec((1, tm, w), lambda b, i: (b, i, 0))
        nxt = pl.BlockSpec((1, FHALO, w), lambda b, i: (b, jnp.minimum((i + 1) * blocks, L // FHALO - 1), 0))
        return [prev, main, nxt]

    consts = (hy_out_g.reshape(1, -1), w_out[:HY_W].astype(BF16), w_out[HY_W:].astype(BF16),
              norm2_g.reshape(1, D), w_up[:, :D_FF].astype(BF16), w_up[:, D_FF:].astype(BF16),
              conv_w[:, :D_FF], conv_w[:, D_FF:], conv_b[:D_FF].reshape(1, -1), conv_b[D_FF:].reshape(1, -1),
              w_down.astype(BF16), final_g.reshape(1, D))
    return pl.pallas_call(
        functools.partial(_ffn_kernel, tm=tm),
        grid=(B, L // tm),
        in_specs=halo3(D) + halo3(HY_W) + halo3(RT_W) + [_const_spec(a.shape) for a in consts],
        out_specs=pl.BlockSpec((1, tm, D), lambda b, i: (b, i, 0)),
        out_shape=jax.ShapeDtypeStruct((B, L, D), F32),
        scratch_shapes=[pltpu.VMEM((tm + 2 * HALO, D_FF), F32)] * 2,
        compiler_params=_params("parallel", "arbitrary"),
        name="mix_ffn",
    )(x, x, x, zh, zh, zh, zr, zr, zr, *consts)


def _trunk(x, prm, spectra, fwd, inv, rot, ret_tables, n):
    hy, rt = _in_proj(x, prm["norm1_g"], prm["w_in"], prm["hy_conv_w"], prm["hy_conv_b"], *rot)
    z = _long_conv(hy, 0, hy, 1, spectra, 0, prm["hy_bias"][0], fwd, inv, n)
    z = _long_conv(z, 0, hy, 2, spectra, 1, prm["hy_bias"][1], fwd, inv, n)
    zr = _retention(rt, ret_tables, prm["ret_gn_g"], prm["ret_gn_b"])
    return _mix_ffn(x, z, zr, prm["hy_out_g"], prm["w_out"], prm["norm2_g"], prm["w_up"], prm["ffn_conv_w"],
                    prm["ffn_conv_b"], prm["w_down"], prm["final_g"])


def _layer(xs, prm, n=CONV_BLOCKS):
    L = xs[0].shape[1]
    assert all(x.shape[1] == L for x in xs) and L % (n * CMAC_ROWS) == 0 and L % (RET_CHUNK * RET_UNROLL) == 0
    fwd, inv = _dft_tables(L // n)
    spectra = _filter_spectra(L, n, fwd, prm["filt_w1"], prm["filt_b1"], prm["filt_w2"], prm["filt_b2"],
                              prm["filt_w3"], prm["filt_b3"], prm["filt_freq"], prm["filt_w4"])
    rot = _rotary_tables(L)
    ret_tables = _retention_tables()
    return tuple(_trunk(x, prm, spectra, fwd, inv, rot, ret_tables, n) for x in xs)


def kernel(x_prompt, x_sample, norm1_g, w_in, hy_conv_w, hy_conv_b, filt_w1, filt_b1, filt_w2, filt_b2,
           filt_w3, filt_b3, filt_freq, filt_w4, hy_bias, hy_out_g, ret_gn_g, ret_gn_b, w_out, norm2_g,
           w_up, ffn_conv_w, ffn_conv_b, w_down, final_g):
    layer = dict(norm1_g=norm1_g, w_in=w_in, hy_conv_w=hy_conv_w, hy_conv_b=hy_conv_b, filt_w1=filt_w1,
                 filt_b1=filt_b1, filt_w2=filt_w2, filt_b2=filt_b2, filt_w3=filt_w3, filt_b3=filt_b3,
                 filt_freq=filt_freq, filt_w4=filt_w4, hy_bias=hy_bias, hy_out_g=hy_out_g,
                 ret_gn_g=ret_gn_g, ret_gn_b=ret_gn_b, w_out=w_out, norm2_g=norm2_g, w_up=w_up,
                 ffn_conv_w=ffn_conv_w, ffn_conv_b=ffn_conv_b, w_down=w_down)
    assert all(v.shape[0] == 1 for v in layer.values())
    prm = {k: v[0] for k, v in layer.items()}
    prm["final_g"] = final_g
    return _layer((x_prompt, x_sample), prm)
```

```python
import functools
import math

import jax
import jax.numpy as jnp
from jax import lax
from jax.experimental import pallas as pl
from jax.experimental.pallas import tpu as pltpu

F32 = jnp.float32
BF16 = jnp.bfloat16

D_MODEL = 1024
HY_W = D_MODEL // 2
RT_W = D_MODEL - HY_W
HY_ORDER = 2
HY_COLS = (HY_ORDER + 1) * HY_W
RT_COLS = 4 * RT_W
FILTER_BANDS = 16
FILTER_HIDDEN = 64
DECAY_TARGET = 1e-2
FAST_DECAY_PCT = 0.3
SLOW_DECAY_PCT = 1.5
RT_HEADS = 8
RT_HEAD_DIM = RT_W // RT_HEADS
ROPE_BASE = 10000.0
DECAY_OFFSET_FWD = 5.0
DECAY_OFFSET_BWD = 5.5
D_FF = ((8 * D_MODEL // 3 + 127) // 128) * 128
EPS = 1e-6

LANES = 128
HALO = 8
FHALO = 16
DFT_SPLIT = 32
CONV_BLOCKS = 4
CONV_CT = 256
ROW_TILE = 512
IN_ROW_TILE = 1024
RET_CHUNK = 256
RET_UNROLL = 2
RET_LANES = 256
CMAC_ROWS = 128
VMEM_LIMIT = 58 * 1024 * 1024


def _params(*sem):
    return pltpu.CompilerParams(dimension_semantics=sem, vmem_limit_bytes=VMEM_LIMIT)


def _dot(a, b):
    return jnp.dot(a, b, preferred_element_type=F32)


def _dot_hi(a, b):
    return jnp.dot(a, b, preferred_element_type=F32, precision=lax.Precision.HIGHEST)


def _const_spec(shape):
    nd = len(shape)
    return pl.BlockSpec(shape, lambda *_: (0,) * nd, pipeline_mode=pl.Buffered(1))


def _dft_tables(P):
    k = jnp.arange(P, dtype=jnp.int32)[:, None]

    def cs(tt):
        ang = (((2 * k + 1) * tt) % (4 * P)).astype(F32) * (math.pi / (2 * P))
        return jnp.cos(ang), jnp.sin(ang)

    ca, sa = cs(DFT_SPLIT * jnp.arange(P // DFT_SPLIT, dtype=jnp.int32)[None, :])
    cb, sb = cs(jnp.arange(DFT_SPLIT, dtype=jnp.int32)[None, :])
    cos = (ca[:, :, None] * cb[:, None, :] - sa[:, :, None] * sb[:, None, :]).reshape(P, P)
    sin = (sa[:, :, None] * cb[:, None, :] + ca[:, :, None] * sb[:, None, :]).reshape(P, P)
    fwd = jnp.concatenate([cos, -sin], axis=0).astype(BF16)
    inv = jnp.concatenate([cos.T, -sin.T], axis=1).astype(BF16)
    return fwd, inv


def _rotary_tables(L):
    inv = ROPE_BASE ** (-jnp.arange(0, RT_HEAD_DIM, 2, dtype=F32) / RT_HEAD_DIM)
    ang = jnp.arange(L, dtype=F32)[:, None] * inv[None, :]
    cos, sin = jnp.cos(ang), jnp.sin(ang)
    reps = LANES // RT_HEAD_DIM
    cos_t = jnp.tile(jnp.concatenate([cos, cos], axis=1), (1, reps))
    sin_t = jnp.tile(jnp.concatenate([-sin, sin], axis=1), (1, reps))
    return cos_t, sin_t


def _retention_tables():
    C = RET_CHUNK
    heads = jnp.arange(RT_HEADS, dtype=F32)
    log_f = jnp.log(1.0 - 2.0 ** (-DECAY_OFFSET_FWD - heads))
    log_b = jnp.log(1.0 - 2.0 ** (-DECAY_OFFSET_BWD - heads))
    idx = jnp.arange(C, dtype=F32)
    diff = idx[:, None] - idx[None, :]
    lower = diff >= 0
    d_f = jnp.where(lower[None], jnp.exp(jnp.where(lower, diff, 0.0)[None] * log_f[:, None, None]), 0.0)
    d_b = jnp.where(~lower[None], jnp.exp(jnp.where(~lower, -diff, 0.0)[None] * log_b[:, None, None]), 0.0)
    dmask = (d_f + d_b).reshape(RT_HEADS // 2, 2 * C, C)

    def lanes(tab):
        rows = tab.shape[1]
        t = jnp.repeat(tab[:, :, None], RT_HEAD_DIM, axis=2)
        t = t.reshape(RT_HEADS // 2, 2, rows, RT_HEAD_DIM).transpose(0, 2, 1, 3)
        return t.reshape(RT_HEADS // 2, rows, LANES)

    qd_f = lanes(jnp.exp((idx + 1.0)[None, :] * log_f[:, None]))
    kd_f = lanes(jnp.exp((C - 1.0 - idx)[None, :] * log_f[:, None]))
    qd_b = lanes(jnp.exp((C - idx)[None, :] * log_b[:, None]))
    kd_b = lanes(jnp.exp(idx[None, :] * log_b[:, None]))
    cd_f = lanes(jnp.exp(C * log_f)[:, None])
    cd_b = lanes(jnp.exp(C * log_b)[:, None])
    return dmask, qd_f, kd_f, qd_b, kd_b, cd_f, cd_b


def _filter_kernel(w1t_ref, w1c_ref, w1s_ref, b1_ref, w2_ref, b2_ref, w3_ref, b3_ref, fr_ref, w4_ref, band_ref,
                   delta_ref, fwd_ref, gpos_ref, gneg_ref, prev_scr, tap_scr, *, P, L):
    j = pl.program_id(0)
    pos_l = (j * P + lax.broadcasted_iota(jnp.int32, (1, P), 1)).astype(F32)
    ang = (2.0 * math.pi / L) * pos_l * band_ref[...]
    freq = fr_ref[...]
    h = (w1t_ref[...] * (pos_l * (1.0 / (L - 1))) + _dot_hi(w1c_ref[...], jnp.cos(ang))
         + _dot_hi(w1s_ref[...], -jnp.sin(ang)) + b1_ref[...])
    h = jnp.sin(freq * h)
    h = jnp.sin(freq * (_dot_hi(w2_ref[...], h) + b2_ref[...]))
    h = jnp.sin(freq * (_dot_hi(w3_ref[...], h) + b3_ref[...]))
    row = lax.broadcasted_iota(jnp.int32, (P, 1), 0)
    t = (j * P + row).astype(F32) * (1.0 / (L - 1))
    sgn = (1 - 2 * (row & 1)).astype(F32)
    lag0 = (row == 0) & (j == 0)
    tn = (((0,), (0,)), ((), ()))
    for c in range(gpos_ref.shape[2] // CONV_CT):
        cs = pl.ds(c * CONV_CT, CONV_CT)
        decay = jnp.exp(-t * delta_ref[:, cs])
        taps = []
        for d in range(2):
            tp = lax.dot_general(h, w4_ref[d, :, cs], tn, preferred_element_type=F32,
                                 precision=lax.Precision.HIGHEST) * decay
            if d == 1:
                tp = jnp.where(lag0, 0.0, tp)
            taps.append(tp.astype(BF16))
        sa = _dot(fwd_ref[...], taps[0])
        sb = _dot(fwd_ref[...], taps[1])

        @pl.when(j == 0)
        def _():
            gpos_ref[0, pl.ds(0, P), cs] = (sa[:P] + sb[:P]).astype(BF16)
            gpos_ref[0, pl.ds(P, P), cs] = (sa[P:] - sb[P:]).astype(BF16)
            gneg_ref[0, :, cs] = jnp.zeros((2 * P, CONV_CT), BF16)

        @pl.when(j > 0)
        def _():
            pa_re = prev_scr[0, pl.ds(0, P), cs] - tap_scr[0:1, cs]
            pa_im = prev_scr[0, pl.ds(P, P), cs]
            gpos_ref[0, pl.ds(0, P), cs] = (sa[:P] - sgn * pa_im).astype(BF16)
            gpos_ref[0, pl.ds(P, P), cs] = (sa[P:] + sgn * pa_re).astype(BF16)
            pb_re = prev_scr[1, pl.ds(0, P), cs] - tap_scr[1:2, cs]
            pb_im = -prev_scr[1, pl.ds(P, P), cs]
            gneg_ref[0, pl.ds(0, P), cs] = (sb[:P] + sgn * pb_im).astype(BF16)
            gneg_ref[0, pl.ds(P, P), cs] = (-sb[P:] - sgn * pb_re).astype(BF16)

        prev_scr[0, :, cs] = sa
        prev_scr[1, :, cs] = sb
        tap_scr[0:1, cs] = taps[0][0:1].astype(F32)
        tap_scr[1:2, cs] = taps[1][0:1].astype(F32)


def _filter_spectra(L, n, fwd, w1, b1, w2, b2, w3, b3, freq, w4):
    P = L // n
    cols = HY_ORDER * HY_W
    bands = jnp.linspace(1e-4, FILTER_BANDS - 1, FILTER_BANDS, dtype=F32)
    deltas = jnp.abs(jnp.linspace(math.log(DECAY_TARGET) / FAST_DECAY_PCT,
                                  math.log(DECAY_TARGET) / SLOW_DECAY_PCT, HY_W, dtype=F32))
    delta = jnp.tile(deltas, HY_ORDER)[None, :]
    w4r = w4.reshape(FILTER_HIDDEN, 2, cols).transpose(1, 0, 2)
    col = lambda a: a.reshape(-1, 1)
    args = (col(w1[0]), w1[1:1 + FILTER_BANDS].T, w1[1 + FILTER_BANDS:].T, col(b1), w2.T, col(b2), w3.T, col(b3),
            col(freq), w4r, col(bands), delta, fwd)
    out = pl.BlockSpec((1, 2 * P, cols), lambda j: (j, 0, 0))
    return pl.pallas_call(
        functools.partial(_filter_kernel, P=P, L=L),
        grid=(n,),
        in_specs=[_const_spec(a.shape) for a in args],
        out_specs=[out, out],
        out_shape=[jax.ShapeDtypeStruct((n, 2 * P, cols), BF16)] * 2,
        scratch_shapes=[pltpu.VMEM((2, 2 * P, cols), F32), pltpu.VMEM((8, cols), F32)],
        compiler_params=_params("arbitrary"),
        name="hyena_filter",
    )(*args)


def _in_kernel(xp_ref, xm_ref, xn_ref, g1_ref, wh_ref, wr_ref, cw_ref, cb_ref, cos_ref, sin_ref,
               hy_ref, rt_ref, *, tm):
    i = pl.program_id(1)
    last = pl.num_programs(1) - 1
    xe = jnp.concatenate([xm_ref[0], xn_ref[0], xp_ref[0]], axis=0)
    xe = xe * lax.rsqrt(jnp.mean(xe * xe, axis=-1, keepdims=True) + EPS) * g1_ref[...]
    row = lax.broadcasted_iota(jnp.int32, (tm + 2 * HALO, 1), 0)
    outside = ((row >= tm + HALO) & (i == 0)) | ((row >= tm) & (row < tm + HALO) & (i == last))
    xe = jnp.where(outside, 0.0, xe).astype(BF16)
    pe = _dot(xe, wh_ref[...])
    hy = (pltpu.roll(pe, 1, 0)[:tm] * cw_ref[0:1, :] + pe[:tm] * cw_ref[1:2, :]
          + pltpu.roll(pe, tm + 2 * HALO - 1, 0)[:tm] * cw_ref[2:3, :] + cb_ref[...])
    hy_ref[0] = hy.astype(BF16)

    pr = _dot(xe[:tm], wr_ref[...])
    cos = cos_ref[...]
    sin = sin_ref[...]
    lane = lax.broadcasted_iota(jnp.int32, (1, LANES), 1)
    first_half = (lane & (RT_HEAD_DIM // 2)) == 0
    for j in range(2 * RT_W // LANES):
        xg = pr[:, j * LANES:(j + 1) * LANES]
        partner = jnp.where(first_half, pltpu.roll(xg, LANES - RT_HEAD_DIM // 2, 1),
                            pltpu.roll(xg, RT_HEAD_DIM // 2, 1))
        r = xg * cos + partner * sin
        if j >= RT_W // LANES:
            r = r * RT_HEAD_DIM ** -0.5
        rt_ref[0, :, j * LANES:(j + 1) * LANES] = r.astype(BF16)
    rt_ref[0, :, 2 * RT_W:] = pr[:, 2 * RT_W:].astype(BF16)


def _halo_specs(tm, L, D):
    blocks = tm // HALO
    prev = pl.BlockSpec((1, HALO, D), lambda b, i: (b, jnp.maximum(i * blocks - 1, 0), 0))
    main = pl.BlockSpec((1, tm, D), lambda b, i: (b, i, 0))
    nxt = pl.BlockSpec((1, HALO, D), lambda b, i: (b, jnp.minimum((i + 1) * blocks, L // HALO - 1), 0))
    return [prev, main, nxt]


def _in_proj(x, g1, w_in, cw, cb, cos_t, sin_t):
    B, L, D = x.shape
    tm = min(IN_ROW_TILE, L)
    wh = w_in[:, :HY_COLS].astype(BF16)
    wr = w_in[:, HY_COLS:].astype(BF16)
    consts = (g1.reshape(1, D), wh, wr, cw, cb.reshape(1, -1))
    return pl.pallas_call(
        functools.partial(_in_kernel, tm=tm),
        grid=(B, L // tm),
        in_specs=_halo_specs(tm, L, D) + [_const_spec(a.shape) for a in consts]
        + [pl.BlockSpec((tm, LANES), lambda b, i: (i, 0))] * 2,
        out_specs=[pl.BlockSpec((1, tm, HY_COLS), lambda b, i: (b, i, 0)),
                   pl.BlockSpec((1, tm, RT_COLS), lambda b, i: (b, i, 0))],
        out_shape=[jax.ShapeDtypeStruct((B, L, HY_COLS), BF16),
                   jax.ShapeDtypeStruct((B, L, RT_COLS), BF16)],
        compiler_params=_params("parallel", "arbitrary"),
        name="in_proj",
    )(x, x, x, *consts, cos_t, sin_t)


def _conv_kernel(u_ref, x1_ref, x2_ref, gp0_ref, gn0_ref, gp1_ref, gn1_ref, bias_ref, fwd_ref, inv_ref, o_ref,
                 spec_scr, prod_scr, z_scr, *, P, n):
    def order(read_u, gate_ref, gpos_ref, gneg_ref, bias, write):
        for j in range(n):
            spec_scr[j] = _dot(fwd_ref[...], read_u(pl.ds(j * P, P))).astype(BF16)
        for i in range(n):
            def rows(t, carry, i=i):
                r_re = pl.ds(pl.multiple_of(t * CMAC_ROWS, CMAC_ROWS), CMAC_ROWS)
                r_im = pl.ds(pl.multiple_of(P + t * CMAC_ROWS, CMAC_ROWS), CMAC_ROWS)
                acc_re = jnp.zeros((CMAC_ROWS, CONV_CT), BF16)
                acc_im = jnp.zeros((CMAC_ROWS, CONV_CT), BF16)
                for j in range(n):
                    g_ref, d = (gpos_ref, i - j) if i >= j else (gneg_ref, j - i)
                    g_re, g_im = g_ref[d, r_re, :], g_ref[d, r_im, :]
                    u_re, u_im = spec_scr[j, r_re, :], spec_scr[j, r_im, :]
                    acc_re += g_re * u_re - g_im * u_im
                    acc_im += g_re * u_im + g_im * u_re
                prod_scr[i, r_re, :] = acc_re
                prod_scr[i, r_im, :] = acc_im
                return carry
            lax.fori_loop(0, P // CMAC_ROWS, rows, 0)
        for i in range(n):
            y = _dot(inv_ref[...], prod_scr[i]) * (1.0 / P)
            blk = pl.ds(i * P, P)
            u = read_u(blk).astype(F32)
            write(blk, (gate_ref[0, blk, :].astype(F32) * (y + u * bias)).astype(BF16))

    def to_scratch(blk, val):
        z_scr[blk, :] = val

    def to_output(blk, val):
        o_ref[0, blk, :] = val

    order(lambda blk: u_ref[0, blk, :], x1_ref, gp0_ref, gn0_ref, bias_ref[0:1, :], to_scratch)
    order(lambda blk: z_scr[blk, :], x2_ref, gp1_ref, gn1_ref, bias_ref[1:2, :], to_output)


def _long_conv(hy, gpos, gneg, bias, fwd, inv, n):
    B, L = hy.shape[:2]
    P = L // n
    nct = HY_W // CONV_CT
    col = lambda which: pl.BlockSpec((1, L, CONV_CT), lambda c, b: (b, 0, which * nct + c))
    spectra = lambda o: pl.BlockSpec((n, 2 * P, CONV_CT), lambda c, b: (0, 0, o * nct + c),
                                     pipeline_mode=pl.Buffered(1))
    return pl.pallas_call(
        functools.partial(_conv_kernel, P=P, n=n),
        grid=(nct, B),
        in_specs=[col(0), col(1), col(2), spectra(0), spectra(0), spectra(1), spectra(1),
                  pl.BlockSpec((HY_ORDER, CONV_CT), lambda c, b: (0, c)),
                  _const_spec(fwd.shape), _const_spec(inv.shape)],
        out_specs=pl.BlockSpec((1, L, CONV_CT), lambda c, b: (b, 0, c)),
        out_shape=jax.ShapeDtypeStruct((B, L, HY_W), BF16),
        scratch_shapes=[pltpu.VMEM((n, 2 * P, CONV_CT), BF16)] * 2 + [pltpu.VMEM((L, CONV_CT), BF16)],
        compiler_params=_params("arbitrary", "arbitrary"),
        name="long_conv",
    )(hy, hy, hy, gpos, gneg, gpos, gneg, bias, fwd, inv)


def _ret_kernel(q_ref, k_ref, v_ref, g_ref, dm_ref, qdf_ref, kdf_ref, qdb_ref, kdb_ref, cdf_ref, cdb_ref,
                avg_ref, gng_ref, gnb_ref, o_ref, acc_scr, kv_scr, sb_scr, *, L):
    C = RET_CHUNK
    pairs = RET_LANES // LANES
    nchunk = L // C
    lane = lax.broadcasted_iota(jnp.int32, (1, LANES), 1)
    head_a = lane < RT_HEAD_DIM
    ri = lax.broadcasted_iota(jnp.int32, (LANES, LANES), 0)
    ci = lax.broadcasted_iota(jnp.int32, (LANES, LANES), 1)
    same_head = ((ri < RT_HEAD_DIM) == (ci < RT_HEAD_DIM)).astype(F32)
    zero = jnp.zeros((), BF16)

    def split_heads(a):
        return jnp.concatenate([jnp.where(head_a, a, zero), jnp.where(head_a, zero, a)], axis=0)

    def kv_body(nidx, carry):
        rows = pl.ds(pl.multiple_of(nidx * C, C), C)
        for p in range(pairs):
            ls = slice(p * LANES, (p + 1) * LANES)
            kc, vc = k_ref[0, rows, ls].astype(F32), v_ref[0, rows, ls]
            kt = jnp.concatenate([(kc * kdf_ref[p]).T, (kc * kdb_ref[p]).T], axis=0).astype(BF16)
            kv_scr[nidx, p] = _dot(kt, vc) * jnp.concatenate([same_head, same_head], axis=0)
        return carry

    def bwd_scan(it, states):
        nidx = nchunk - 1 - it
        new = []
        for p in range(pairs):
            sb_scr[nidx, p] = states[p].astype(BF16)
            new.append(states[p] * cdb_ref[p] + kv_scr[nidx, p, pl.ds(LANES, LANES), :])
        return tuple(new)

    def main_body(nidx, states):
        rows = pl.ds(pl.multiple_of(nidx * C, C), C)
        new = []
        for p in range(pairs):
            ls = slice(p * LANES, (p + 1) * LANES)
            qc, kc, vc = q_ref[0, rows, ls], k_ref[0, rows, ls], v_ref[0, rows, ls]
            s = lax.dot_general(split_heads(qc), kc, (((1,), (1,)), ((), ())), preferred_element_type=F32)
            s = (s * dm_ref[p]).astype(BF16)
            qf = (qc.astype(F32) * qdf_ref[p]).astype(BF16)
            qb = (qc.astype(F32) * qdb_ref[p]).astype(BF16)
            lhs = jnp.concatenate([s[:C], s[C:], qf, qb], axis=1)
            rhs = jnp.concatenate([split_heads(vc), states[p].astype(BF16), sb_scr[nidx, p]], axis=0)
            acc_scr[rows, ls] = _dot(lhs, rhs)
            new.append(states[p] * cdf_ref[p] + kv_scr[nidx, p, pl.ds(0, LANES), :])
        return tuple(new)

    init = tuple(jnp.zeros((LANES, LANES), F32) for _ in range(pairs))
    lax.fori_loop(0, nchunk, kv_body, 0, unroll=RET_UNROLL)
    lax.fori_loop(0, nchunk, bwd_scan, init)
    lax.fori_loop(0, nchunk, main_body, init, unroll=RET_UNROLL)

    tn = min(ROW_TILE, L)
    avg = avg_ref[...]

    def group_mean(a):
        hi = a.astype(BF16)
        lo = (a - hi.astype(F32)).astype(BF16)
        return _dot(hi, avg) + _dot(lo, avg)

    def norm_body(it, carry):
        rows = pl.ds(pl.multiple_of(it * tn, tn), tn)
        o = acc_scr[rows, :]
        oc = o - group_mean(o)
        var = _dot((oc * oc).astype(BF16), avg)
        on = oc * lax.rsqrt(var + EPS) * gng_ref[...] + gnb_ref[...]
        g = g_ref[0, rows, :].astype(F32)
        o_ref[0, rows, :] = (on * g * (0.5 + 0.5 * jnp.tanh(0.5 * g))).astype(BF16)
        return carry

    lax.fori_loop(0, L // tn, norm_body, 0, unroll=2)


def _retention(rt, tables, gn_g, gn_b):
    B, L, _ = rt.shape
    groups = RT_W // RET_LANES
    pairs = RET_LANES // LANES
    li = jnp.arange(RET_LANES) // RT_HEAD_DIM
    avg = ((li[:, None] == li[None, :]).astype(F32) / RT_HEAD_DIM).astype(BF16)
    col = lambda which: pl.BlockSpec((1, L, RET_LANES), lambda g, b: (b, 0, which * groups + g))
    tab = lambda a: pl.BlockSpec((pairs,) + a.shape[1:], lambda g, b: (g, 0, 0))
    vec = pl.BlockSpec((1, RET_LANES), lambda g, b: (0, g))
    return pl.pallas_call(
        functools.partial(_ret_kernel, L=L),
        grid=(groups, B),
        in_specs=[col(0), col(1), col(2), col(3)] + [tab(a) for a in tables] + [_const_spec(avg.shape), vec, vec],
        out_specs=pl.BlockSpec((1, L, RET_LANES), lambda g, b: (b, 0, g)),
        out_shape=jax.ShapeDtypeStruct((B, L, RT_W), BF16),
        scratch_shapes=[pltpu.VMEM((L, RET_LANES), F32),
                        pltpu.VMEM((L // RET_CHUNK, pairs, 2 * LANES, LANES), F32),
                        pltpu.VMEM((L // RET_CHUNK, pairs, LANES, LANES), BF16)],
        compiler_params=_params("arbitrary", "arbitrary"),
        name="retention",
    )(rt, rt, rt, rt, *tables, avg, gn_g.reshape(1, -1), gn_b.reshape(1, -1))


def _gelu_tanh(a):
    return 0.5 * a * (1.0 + jnp.tanh(math.sqrt(2.0 / math.pi) * (a + 0.044715 * (a * a * a))))


def _rms(x, g):
    return x * lax.rsqrt(jnp.mean(x * x, axis=-1, keepdims=True) + EPS) * g


def _ffn_kernel(xp_ref, xm_ref, xn_ref, hp_ref, hm_ref, hn_ref, rp_ref, rm_ref, rn_ref, hg_ref, woh_ref, wor_ref,
                g2_ref, wa_ref, wb_ref, cwa_ref, cwb_ref, cba_ref, cbb_ref, wd_ref, gf_ref, o_ref,
                ua_scr, ub_scr, *, tm):
    i = pl.program_id(1)
    last = pl.num_programs(1) - 1
    def cat(p, m, n):
        return jnp.concatenate([p[0].astype(F32)[FHALO - HALO:], m[0].astype(F32), n[0].astype(F32)[:HALO]], axis=0)

    yh = _rms(cat(hp_ref, hm_ref, hn_ref), hg_ref[...]).astype(BF16)
    x1 = (cat(xp_ref, xm_ref, xn_ref) + _dot(yh, woh_ref[...])
          + _dot(cat(rp_ref, rm_ref, rn_ref).astype(BF16), wor_ref[...]))
    xe = _rms(x1, g2_ref[...])
    row = lax.broadcasted_iota(jnp.int32, (tm + 2 * HALO, 1), 0)
    outside = ((row < HALO) & (i == 0)) | ((row >= tm + HALO) & (i == last))
    xe = jnp.where(outside, 0.0, xe).astype(BF16)

    def conv3(scr, cw_ref, cb_ref):
        return (scr[pl.ds(HALO - 1, tm), :] * cw_ref[0:1, :] + scr[pl.ds(HALO, tm), :] * cw_ref[1:2, :]
                + scr[pl.ds(HALO + 1, tm), :] * cw_ref[2:3, :] + cb_ref[...])

    ua_scr[...] = _dot(xe, wa_ref[...])
    ub_scr[...] = _dot(xe, wb_ref[...])
    gated = _gelu_tanh(conv3(ua_scr, cwa_ref, cba_ref)) * conv3(ub_scr, cwb_ref, cbb_ref)
    y = x1[HALO:HALO + tm] + _dot(gated.astype(BF16), wd_ref[...])
    o_ref[0] = _rms(y, gf_ref[...])


def _mix_ffn(x, zh, zr, hy_out_g, w_out, norm2_g, w_up, conv_w, conv_b, w_down, final_g):
    B, L, D = x.shape
    tm = min(ROW_TILE, L)
    blocks = tm // FHALO

    def halo3(w):
        prev = pl.BlockSpec((1, FHALO, w), lambda b, i: (b, jnp.maximum(i * blocks - 1, 0), 0))
        main = pl.BlockSpec((1, tm, w), lambda b, i: (b, i, 0))
        nxt = pl.BlockSpec((1, FHALO, w), lambda b, i: (b, jnp.minimum((i + 1) * blocks, L // FHALO - 1), 0))
        return [prev, main, nxt]

    consts = (hy_out_g.reshape(1, -1), w_out[:HY_W].astype(BF16), w_out[HY_W:].astype(BF16),
              norm2_g.reshape(1, D), w_up[:, :D_FF].astype(BF16), w_up[:, D_FF:].astype(BF16),
              conv_w[:, :D_FF], conv_w[:, D_FF:], conv_b[:D_FF].reshape(1, -1), conv_b[D_FF:].reshape(1, -1),
              w_down.astype(BF16), final_g.reshape(1, D))
    return pl.pallas_call(
        functools.partial(_ffn_kernel, tm=tm),
        grid=(B, L // tm),
        in_specs=halo3(D) + halo3(HY_W) + halo3(RT_W) + [_const_spec(a.shape) for a in consts],
        out_specs=pl.BlockSpec((1, tm, D), lambda b, i: (b, i, 0)),
        out_shape=jax.ShapeDtypeStruct((B, L, D), F32),
        scratch_shapes=[pltpu.VMEM((tm + 2 * HALO, D_FF), F32)] * 2,
        compiler_params=_params("parallel", "arbitrary"),
        name="mix_ffn",
    )(x, x, x, zh, zh, zh, zr, zr, zr, *consts)


def _trunk(x, prm, spectra, fwd, inv, rot, ret_tables, n):
    hy, rt = _in_proj(x, prm["norm1_g"], prm["w_in"], prm["hy_conv_w"], prm["hy_conv_b"], *rot)
    z = _long_conv(hy, *spectra, prm["hy_bias"], fwd, inv, n)
    zr = _retention(rt, ret_tables, prm["ret_gn_g"], prm["ret_gn_b"])
    return _mix_ffn(x, z, zr, prm["hy_out_g"], prm["w_out"], prm["norm2_g"], prm["w_up"], prm["ffn_conv_w"],
                    prm["ffn_conv_b"], prm["w_down"], prm["final_g"])


def _layer(xs, prm, n=CONV_BLOCKS):
    L = xs[0].shape[1]
    assert all(x.shape[1] == L for x in xs) and L % (n * CMAC_ROWS) == 0 and L % (RET_CHUNK * RET_UNROLL) == 0
    fwd, inv = _dft_tables(L // n)
    spectra = _filter_spectra(L, n, fwd, prm["filt_w1"], prm["filt_b1"], prm["filt_w2"], prm["filt_b2"],
                              prm["filt_w3"], prm["filt_b3"], prm["filt_freq"], prm["filt_w4"])
    rot = _rotary_tables(L)
    ret_tables = _retention_tables()
    return tuple(_trunk(x, prm, spectra, fwd, inv, rot, ret_tables, n) for x in xs)


def kernel(x_prompt, x_sample, norm1_g, w_in, hy_conv_w, hy_conv_b, filt_w1, filt_b1, filt_w2, filt_b2,
           filt_w3, filt_b3, filt_freq, filt_w4, hy_bias, hy_out_g, ret_gn_g, ret_gn_b, w_out, norm2_g,
           w_up, ffn_conv_w, ffn_conv_b, w_down, final_g):
    layer = dict(norm1_g=norm1_g, w_in=w_in, hy_conv_w=hy_conv_w, hy_conv_b=hy_conv_b, filt_w1=filt_w1,
                 filt_b1=filt_b1, filt_w2=filt_w2, filt_b2=filt_b2, filt_w3=filt_w3, filt_b3=filt_b3,
                 filt_freq=filt_freq, filt_w4=filt_w4, hy_bias=hy_bias, hy_out_g=hy_out_g,
                 ret_gn_g=ret_gn_g, ret_gn_b=ret_gn_b, w_out=w_out, norm2_g=norm2_g, w_up=w_up,
                 ffn_conv_w=ffn_conv_w, ffn_conv_b=ffn_conv_b, w_down=w_down)
    assert all(v.shape[0] == 1 for v in layer.values())
    prm = {k: v[0] for k, v in layer.items()}
    prm["final_g"] = final_g
    return _layer((x_prompt, x_sample), prm)
```

```python
import functools
import math

import jax
import jax.numpy as jnp
from jax import lax
from jax.experimental import pallas as pl
from jax.experimental.pallas import tpu as pltpu

F32 = jnp.float32
BF16 = jnp.bfloat16

D_MODEL = 1024
HY_W = D_MODEL // 2
RT_W = D_MODEL - HY_W
HY_ORDER = 2
HY_COLS = (HY_ORDER + 1) * HY_W
RT_COLS = 4 * RT_W
FILTER_BANDS = 16
FILTER_HIDDEN = 64
DECAY_TARGET = 1e-2
FAST_DECAY_PCT = 0.3
SLOW_DECAY_PCT = 1.5
RT_HEADS = 8
RT_HEAD_DIM = RT_W // RT_HEADS
ROPE_BASE = 10000.0
DECAY_OFFSET_FWD = 5.0
DECAY_OFFSET_BWD = 5.5
D_FF = ((8 * D_MODEL // 3 + 127) // 128) * 128
EPS = 1e-6

LANES = 128
HALO = 8
FHALO = 16
DFT_SPLIT = 32
CONV_BLOCKS = 4
CONV_CT = 256
ROW_TILE = 512
IN_ROW_TILE = 1024
RET_CHUNK = 256
RET_UNROLL = 2
RET_LANES = 256
CMAC_ROWS = 128
VMEM_LIMIT = 58 * 1024 * 1024


def _params(*sem):
    return pltpu.CompilerParams(dimension_semantics=sem, vmem_limit_bytes=VMEM_LIMIT)


def _dot(a, b):
    return jnp.dot(a, b, preferred_element_type=F32)


def _dot_hi(a, b):
    return jnp.dot(a, b, preferred_element_type=F32, precision=lax.Precision.HIGHEST)


def _const_spec(shape):
    nd = len(shape)
    return pl.BlockSpec(shape, lambda *_: (0,) * nd, pipeline_mode=pl.Buffered(1))


def _dft_tables(P):
    k = jnp.arange(P, dtype=jnp.int32)[:, None]

    def cs(tt):
        ang = (((2 * k + 1) * tt) % (4 * P)).astype(F32) * (math.pi / (2 * P))
        return jnp.cos(ang), jnp.sin(ang)

    ca, sa = cs(DFT_SPLIT * jnp.arange(P // DFT_SPLIT, dtype=jnp.int32)[None, :])
    cb, sb = cs(jnp.arange(DFT_SPLIT, dtype=jnp.int32)[None, :])
    cos = (ca[:, :, None] * cb[:, None, :] - sa[:, :, None] * sb[:, None, :]).reshape(P, P)
    sin = (sa[:, :, None] * cb[:, None, :] + ca[:, :, None] * sb[:, None, :]).reshape(P, P)
    fwd = jnp.concatenate([cos, -sin], axis=0).astype(BF16)
    inv = jnp.concatenate([cos.T, -sin.T], axis=1).astype(BF16)
    return fwd, inv


def _rotary_tables(L):
    inv = ROPE_BASE ** (-jnp.arange(0, RT_HEAD_DIM, 2, dtype=F32) / RT_HEAD_DIM)
    ang = jnp.arange(L, dtype=F32)[:, None] * inv[None, :]
    cos, sin = jnp.cos(ang), jnp.sin(ang)
    reps = LANES // RT_HEAD_DIM
    cos_t = jnp.tile(jnp.concatenate([cos, cos], axis=1), (1, reps))
    sin_t = jnp.tile(jnp.concatenate([-sin, sin], axis=1), (1, reps))
    return cos_t, sin_t


def _retention_tables():
    C = RET_CHUNK
    heads = jnp.arange(RT_HEADS, dtype=F32)
    log_f = jnp.log(1.0 - 2.0 ** (-DECAY_OFFSET_FWD - heads))
    log_b = jnp.log(1.0 - 2.0 ** (-DECAY_OFFSET_BWD - heads))
    idx = jnp.arange(C, dtype=F32)
    diff = idx[:, None] - idx[None, :]
    lower = diff >= 0
    d_f = jnp.where(lower[None], jnp.exp(jnp.where(lower, diff, 0.0)[None] * log_f[:, None, None]), 0.0)
    d_b = jnp.where(~lower[None], jnp.exp(jnp.where(~lower, -diff, 0.0)[None] * log_b[:, None, None]), 0.0)
    dmask = (d_f + d_b).reshape(RT_HEADS // 2, 2 * C, C)

    def lanes(tab):
        rows = tab.shape[1]
        t = jnp.repeat(tab[:, :, None], RT_HEAD_DIM, axis=2)
        t = t.reshape(RT_HEADS // 2, 2, rows, RT_HEAD_DIM).transpose(0, 2, 1, 3)
        return t.reshape(RT_HEADS // 2, rows, LANES)

    qd_f = lanes(jnp.exp((idx + 1.0)[None, :] * log_f[:, None]))
    kd_f = lanes(jnp.exp((C - 1.0 - idx)[None, :] * log_f[:, None]))
    qd_b = lanes(jnp.exp((C - idx)[None, :] * log_b[:, None]))
    kd_b = lanes(jnp.exp(idx[None, :] * log_b[:, None]))
    cd_f = lanes(jnp.exp(C * log_f)[:, None])
    cd_b = lanes(jnp.exp(C * log_b)[:, None])
    return dmask, qd_f, kd_f, qd_b, kd_b, cd_f, cd_b


def _filter_kernel(w1t_ref, w1c_ref, w1s_ref, b1_ref, w2_ref, b2_ref, w3_ref, b3_ref, fr_ref, w4_ref, band_ref,
                   delta_ref, fwd_ref, gpos_ref, gneg_ref, prev_scr, tap_scr, *, P, L):
    j = pl.program_id(0)
    pos_l = (j * P + lax.broadcasted_iota(jnp.int32, (1, P), 1)).astype(F32)
    ang = (2.0 * math.pi / L) * pos_l * band_ref[...]
    freq = fr_ref[...]
    h = (w1t_ref[...] * (pos_l * (1.0 / (L - 1))) + _dot_hi(w1c_ref[...], jnp.cos(ang))
         + _dot_hi(w1s_ref[...], -jnp.sin(ang)) + b1_ref[...])
    h = jnp.sin(freq * h)
    h = jnp.sin(freq * (_dot_hi(w2_ref[...], h) + b2_ref[...]))
    h = jnp.sin(freq * (_dot_hi(w3_ref[...], h) + b3_ref[...]))
    row = lax.broadcasted_iota(jnp.int32, (P, 1), 0)
    t = (j * P + row).astype(F32) * (1.0 / (L - 1))
    sgn = (1 - 2 * (row & 1)).astype(F32)
    lag0 = (row == 0) & (j == 0)
    tn = (((0,), (0,)), ((), ()))
    for c in range(gpos_ref.shape[2] // CONV_CT):
        cs = pl.ds(c * CONV_CT, CONV_CT)
        decay = jnp.exp(-t * delta_ref[:, cs])
        taps = []
        for d in range(2):
            tp = lax.dot_general(h, w4_ref[d, :, cs], tn, preferred_element_type=F32,
                                 precision=lax.Precision.HIGHEST) * decay
            if d == 1:
                tp = jnp.where(lag0, 0.0, tp)
            taps.append(tp.astype(BF16))
        sa = _dot(fwd_ref[...], taps[0])
        sb = _dot(fwd_ref[...], taps[1])

        @pl.when(j == 0)
        def _():
            gpos_ref[0, pl.ds(0, P), cs] = (sa[:P] + sb[:P]).astype(BF16)
            gpos_ref[0, pl.ds(P, P), cs] = (sa[P:] - sb[P:]).astype(BF16)
            gneg_ref[0, :, cs] = jnp.zeros((2 * P, CONV_CT), BF16)

        @pl.when(j > 0)
        def _():
            pa_re = prev_scr[0, pl.ds(0, P), cs] - tap_scr[0:1, cs]
            pa_im = prev_scr[0, pl.ds(P, P), cs]
            gpos_ref[0, pl.ds(0, P), cs] = (sa[:P] - sgn * pa_im).astype(BF16)
            gpos_ref[0, pl.ds(P, P), cs] = (sa[P:] + sgn * pa_re).astype(BF16)
            pb_re = prev_scr[1, pl.ds(0, P), cs] - tap_scr[1:2, cs]
            pb_im = -prev_scr[1, pl.ds(P, P), cs]
            gneg_ref[0, pl.ds(0, P), cs] = (sb[:P] + sgn * pb_im).astype(BF16)
            gneg_ref[0, pl.ds(P, P), cs] = (-sb[P:] - sgn * pb_re).astype(BF16)

        prev_scr[0, :, cs] = sa
        prev_scr[1, :, cs] = sb
        tap_scr[0:1, cs] = taps[0][0:1].astype(F32)
        tap_scr[1:2, cs] = taps[1][0:1].astype(F32)


def _filter_spectra(L, n, fwd, w1, b1, w2, b2, w3, b3, freq, w4):
    P = L // n
    cols = HY_ORDER * HY_W
    bands = jnp.linspace(1e-4, FILTER_BANDS - 1, FILTER_BANDS, dtype=F32)
    deltas = jnp.abs(jnp.linspace(math.log(DECAY_TARGET) / FAST_DECAY_PCT,
                                  math.log(DECAY_TARGET) / SLOW_DECAY_PCT, HY_W, dtype=F32))
    delta = jnp.tile(deltas, HY_ORDER)[None, :]
    w4r = w4.reshape(FILTER_HIDDEN, 2, cols).transpose(1, 0, 2)
    col = lambda a: a.reshape(-1, 1)
    args = (col(w1[0]), w1[1:1 + FILTER_BANDS].T, w1[1 + FILTER_BANDS:].T, col(b1), w2.T, col(b2), w3.T, col(b3),
            col(freq), w4r, col(bands), delta, fwd)
    out = pl.BlockSpec((1, 2 * P, cols), lambda j: (j, 0, 0))
    return pl.pallas_call(
        functools.partial(_filter_kernel, P=P, L=L),
        grid=(n,),
        in_specs=[_const_spec(a.shape) for a in args],
        out_specs=[out, out],
        out_shape=[jax.ShapeDtypeStruct((n, 2 * P, cols), BF16)] * 2,
        scratch_shapes=[pltpu.VMEM((2, 2 * P, cols), F32), pltpu.VMEM((8, cols), F32)],
        compiler_params=_params("arbitrary"),
        name="hyena_filter",
    )(*args)


def _in_kernel(xp_ref, xm_ref, xn_ref, g1_ref, wh_ref, wr_ref, cw_ref, cb_ref, cos_ref, sin_ref,
               hy_ref, rt_ref, *, tm):
    i = pl.program_id(1)
    last = pl.num_programs(1) - 1
    xe = jnp.concatenate([xm_ref[0], xn_ref[0], xp_ref[0]], axis=0)
    xe = xe * lax.rsqrt(jnp.mean(xe * xe, axis=-1, keepdims=True) + EPS) * g1_ref[...]
    row = lax.broadcasted_iota(jnp.int32, (tm + 2 * HALO, 1), 0)
    outside = ((row >= tm + HALO) & (i == 0)) | ((row >= tm) & (row < tm + HALO) & (i == last))
    xe = jnp.where(outside, 0.0, xe).astype(BF16)
    pe = _dot(xe, wh_ref[...])
    hy = (pltpu.roll(pe, 1, 0)[:tm] * cw_ref[0:1, :] + pe[:tm] * cw_ref[1:2, :]
          + pltpu.roll(pe, tm + 2 * HALO - 1, 0)[:tm] * cw_ref[2:3, :] + cb_ref[...])
    hy_ref[0] = hy.astype(BF16)

    pr = _dot(xe[:tm], wr_ref[...])
    cos = cos_ref[...]
    sin = sin_ref[...]
    lane = lax.broadcasted_iota(jnp.int32, (1, LANES), 1)
    first_half = (lane & (RT_HEAD_DIM // 2)) == 0
    for j in range(2 * RT_W // LANES):
        xg = pr[:, j * LANES:(j + 1) * LANES]
        partner = jnp.where(first_half, pltpu.roll(xg, LANES - RT_HEAD_DIM // 2, 1),
                            pltpu.roll(xg, RT_HEAD_DIM // 2, 1))
        r = xg * cos + partner * sin
        if j >= RT_W // LANES:
            r = r * RT_HEAD_DIM ** -0.5
        rt_ref[0, :, j * LANES:(j + 1) * LANES] = r.astype(BF16)
    rt_ref[0, :, 2 * RT_W:] = pr[:, 2 * RT_W:].astype(BF16)


def _halo_specs(tm, L, D):
    blocks = tm // HALO
    prev = pl.BlockSpec((1, HALO, D), lambda b, i: (b, jnp.maximum(i * blocks - 1, 0), 0))
    main = pl.BlockSpec((1, tm, D), lambda b, i: (b, i, 0))
    nxt = pl.BlockSpec((1, HALO, D), lambda b, i: (b, jnp.minimum((i + 1) * blocks, L // HALO - 1), 0))
    return [prev, main, nxt]


def _in_proj(x, g1, w_in, cw, cb, cos_t, sin_t):
    B, L, D = x.shape
    tm = min(IN_ROW_TILE, L)
    wh = w_in[:, :HY_COLS].astype(BF16)
    wr = w_in[:, HY_COLS:].astype(BF16)
    consts = (g1.reshape(1, D), wh, wr, cw, cb.reshape(1, -1))
    return pl.pallas_call(
        functools.partial(_in_kernel, tm=tm),
        grid=(B, L // tm),
        in_specs=_halo_specs(tm, L, D) + [_const_spec(a.shape) for a in consts]
        + [pl.BlockSpec((tm, LANES), lambda b, i: (i, 0))] * 2,
        out_specs=[pl.BlockSpec((1, tm, HY_COLS), lambda b, i: (b, i, 0)),
                   pl.BlockSpec((1, tm, RT_COLS), lambda b, i: (b, i, 0))],
        out_shape=[jax.ShapeDtypeStruct((B, L, HY_COLS), BF16),
                   jax.ShapeDtypeStruct((B, L, RT_COLS), BF16)],
        compiler_params=_params("parallel", "arbitrary"),
        name="in_proj",
    )(x, x, x, *consts, cos_t, sin_t)


def _conv_kernel(u_ref, x1_ref, x2_ref, gp0_ref, gn0_ref, gp1_ref, gn1_ref, bias_ref, fwd_ref, inv_ref, o_ref,
                 spec_scr, prod_scr, z_scr, *, P, n):
    def order(read_u, gate_ref, gpos_ref, gneg_ref, bias, write):
        for j in range(n):
            spec_scr[j] = _dot(fwd_ref[...], read_u(pl.ds(j * P, P))).astype(BF16)
        for i in range(n):
            for t in range(P // CMAC_ROWS):
                r_re = pl.ds(t * CMAC_ROWS, CMAC_ROWS)
                r_im = pl.ds(P + t * CMAC_ROWS, CMAC_ROWS)
                acc_re = jnp.zeros((CMAC_ROWS, CONV_CT), BF16)
                acc_im = jnp.zeros((CMAC_ROWS, CONV_CT), BF16)
                for j in range(n):
                    g_ref, d = (gpos_ref, i - j) if i >= j else (gneg_ref, j - i)
                    g_re, g_im = g_ref[d, r_re, :], g_ref[d, r_im, :]
                    u_re, u_im = spec_scr[j, r_re, :], spec_scr[j, r_im, :]
                    acc_re += g_re * u_re - g_im * u_im
                    acc_im += g_re * u_im + g_im * u_re
                prod_scr[i, r_re, :] = acc_re
                prod_scr[i, r_im, :] = acc_im
        for i in range(n):
            y = _dot(inv_ref[...], prod_scr[i]) * (1.0 / P)
            blk = pl.ds(i * P, P)
            u = read_u(blk).astype(F32)
            write(blk, (gate_ref[0, blk, :].astype(F32) * (y + u * bias)).astype(BF16))

    def to_scratch(blk, val):
        z_scr[blk, :] = val

    def to_output(blk, val):
        o_ref[0, blk, :] = val

    order(lambda blk: u_ref[0, blk, :], x1_ref, gp0_ref, gn0_ref, bias_ref[0:1, :], to_scratch)
    order(lambda blk: z_scr[blk, :], x2_ref, gp1_ref, gn1_ref, bias_ref[1:2, :], to_output)


def _long_conv(hy, gpos, gneg, bias, fwd, inv, n):
    B, L = hy.shape[:2]
    P = L // n
    nct = HY_W // CONV_CT
    col = lambda which: pl.BlockSpec((1, L, CONV_CT), lambda c, b: (b, 0, which * nct + c))
    spectra = lambda o: pl.BlockSpec((n, 2 * P, CONV_CT), lambda c, b: (0, 0, o * nct + c),
                                     pipeline_mode=pl.Buffered(1))
    return pl.pallas_call(
        functools.partial(_conv_kernel, P=P, n=n),
        grid=(nct, B),
        in_specs=[col(0), col(1), col(2), spectra(0), spectra(0), spectra(1), spectra(1),
                  pl.BlockSpec((HY_ORDER, CONV_CT), lambda c, b: (0, c)),
                  _const_spec(fwd.shape), _const_spec(inv.shape)],
        out_specs=pl.BlockSpec((1, L, CONV_CT), lambda c, b: (b, 0, c)),
        out_shape=jax.ShapeDtypeStruct((B, L, HY_W), BF16),
        scratch_shapes=[pltpu.VMEM((n, 2 * P, CONV_CT), BF16)] * 2 + [pltpu.VMEM((L, CONV_CT), BF16)],
        compiler_params=_params("arbitrary", "arbitrary"),
        name="long_conv",
    )(hy, hy, hy, gpos, gneg, gpos, gneg, bias, fwd, inv)


def _ret_kernel(q_ref, k_ref, v_ref, g_ref, dm_ref, qdf_ref, kdf_ref, qdb_ref, kdb_ref, cdf_ref, cdb_ref,
                avg_ref, gng_ref, gnb_ref, o_ref, acc_scr, kv_scr, sb_scr, *, L):
    C = RET_CHUNK
    pairs = RET_LANES // LANES
    nchunk = L // C
    lane = lax.broadcasted_iota(jnp.int32, (1, LANES), 1)
    head_a = lane < RT_HEAD_DIM
    ri = lax.broadcasted_iota(jnp.int32, (LANES, LANES), 0)
    ci = lax.broadcasted_iota(jnp.int32, (LANES, LANES), 1)
    same_head = ((ri < RT_HEAD_DIM) == (ci < RT_HEAD_DIM)).astype(F32)
    zero = jnp.zeros((), BF16)

    def split_heads(a):
        return jnp.concatenate([jnp.where(head_a, a, zero), jnp.where(head_a, zero, a)], axis=0)

    def kv_body(nidx, carry):
        rows = pl.ds(pl.multiple_of(nidx * C, C), C)
        for p in range(pairs):
            ls = slice(p * LANES, (p + 1) * LANES)
            kc, vc = k_ref[0, rows, ls].astype(F32), v_ref[0, rows, ls]
            kt = jnp.concatenate([(kc * kdf_ref[p]).T, (kc * kdb_ref[p]).T], axis=0).astype(BF16)
            kv_scr[nidx, p] = _dot(kt, vc) * jnp.concatenate([same_head, same_head], axis=0)
        return carry

    def bwd_scan(it, states):
        nidx = nchunk - 1 - it
        new = []
        for p in range(pairs):
            sb_scr[nidx, p] = states[p].astype(BF16)
            new.append(states[p] * cdb_ref[p] + kv_scr[nidx, p, pl.ds(LANES, LANES), :])
        return tuple(new)

    def main_body(nidx, states):
        rows = pl.ds(pl.multiple_of(nidx * C, C), C)
        new = []
        for p in range(pairs):
            ls = slice(p * LANES, (p + 1) * LANES)
            qc, kc, vc = q_ref[0, rows, ls], k_ref[0, rows, ls], v_ref[0, rows, ls]
            s = lax.dot_general(split_heads(qc), kc, (((1,), (1,)), ((), ())), preferred_element_type=F32)
            s = (s * dm_ref[p]).astype(BF16)
            qf = (qc.astype(F32) * qdf_ref[p]).astype(BF16)
            qb = (qc.astype(F32) * qdb_ref[p]).astype(BF16)
            lhs = jnp.concatenate([s[:C], s[C:], qf, qb], axis=1)
            rhs = jnp.concatenate([split_heads(vc), states[p].astype(BF16), sb_scr[nidx, p]], axis=0)
            acc_scr[rows, ls] = _dot(lhs, rhs)
            new.append(states[p] * cdf_ref[p] + kv_scr[nidx, p, pl.ds(0, LANES), :])
        return tuple(new)

    init = tuple(jnp.zeros((LANES, LANES), F32) for _ in range(pairs))
    lax.fori_loop(0, nchunk, kv_body, 0, unroll=RET_UNROLL)
    lax.fori_loop(0, nchunk, bwd_scan, init)
    lax.fori_loop(0, nchunk, main_body, init, unroll=RET_UNROLL)

    tn = min(ROW_TILE, L)
    avg = avg_ref[...]

    def group_mean(a):
        hi = a.astype(BF16)
        lo = (a - hi.astype(F32)).astype(BF16)
        return _dot(hi, avg) + _dot(lo, avg)

    def norm_body(it, carry):
        rows = pl.ds(pl.multiple_of(it * tn, tn), tn)
        o = acc_scr[rows, :]
        oc = o - group_mean(o)
        var = _dot((oc * oc).astype(BF16), avg)
        on = oc * lax.rsqrt(var + EPS) * gng_ref[...] + gnb_ref[...]
        g = g_ref[0, rows, :].astype(F32)
        o_ref[0, rows, :] = (on * g * (0.5 + 0.5 * jnp.tanh(0.5 * g))).astype(BF16)
        return carry

    lax.fori_loop(0, L // tn, norm_body, 0, unroll=2)


def _retention(rt, tables, gn_g, gn_b):
    B, L, _ = rt.shape
    groups = RT_W // RET_LANES
    pairs = RET_LANES // LANES
    li = jnp.arange(RET_LANES) // RT_HEAD_DIM
    avg = ((li[:, None] == li[None, :]).astype(F32) / RT_HEAD_DIM).astype(BF16)
    col = lambda which: pl.BlockSpec((1, L, RET_LANES), lambda g, b: (b, 0, which * groups + g))
    tab = lambda a: pl.BlockSpec((pairs,) + a.shape[1:], lambda g, b: (g, 0, 0))
    vec = pl.BlockSpec((1, RET_LANES), lambda g, b: (0, g))
    return pl.pallas_call(
        functools.partial(_ret_kernel, L=L),
        grid=(groups, B),
        in_specs=[col(0), col(1), col(2), col(3)] + [tab(a) for a in tables] + [_const_spec(avg.shape), vec, vec],
        out_specs=pl.BlockSpec((1, L, RET_LANES), lambda g, b: (b, 0, g)),
        out_shape=jax.ShapeDtypeStruct((B, L, RT_W), BF16),
        scratch_shapes=[pltpu.VMEM((L, RET_LANES), F32),
                        pltpu.VMEM((L // RET_CHUNK, pairs, 2 * LANES, LANES), F32),
                        pltpu.VMEM((L // RET_CHUNK, pairs, LANES, LANES), BF16)],
        compiler_params=_params("arbitrary", "arbitrary"),
        name="retention",
    )(rt, rt, rt, rt, *tables, avg, gn_g.reshape(1, -1), gn_b.reshape(1, -1))


def _gelu_tanh(a):
    return 0.5 * a * (1.0 + jnp.tanh(math.sqrt(2.0 / math.pi) * (a + 0.044715 * (a * a * a))))


def _rms(x, g):
    return x * lax.rsqrt(jnp.mean(x * x, axis=-1, keepdims=True) + EPS) * g


def _ffn_kernel(xp_ref, xm_ref, xn_ref, hp_ref, hm_ref, hn_ref, rp_ref, rm_ref, rn_ref, hg_ref, woh_ref, wor_ref,
                g2_ref, wa_ref, wb_ref, cwa_ref, cwb_ref, cba_ref, cbb_ref, wd_ref, gf_ref, o_ref,
                ua_scr, ub_scr, *, tm):
    i = pl.program_id(1)
    last = pl.num_programs(1) - 1
    def cat(p, m, n):
        return jnp.concatenate([p[0].astype(F32)[FHALO - HALO:], m[0].astype(F32), n[0].astype(F32)[:HALO]], axis=0)

    yh = _rms(cat(hp_ref, hm_ref, hn_ref), hg_ref[...]).astype(BF16)
    x1 = (cat(xp_ref, xm_ref, xn_ref) + _dot(yh, woh_ref[...])
          + _dot(cat(rp_ref, rm_ref, rn_ref).astype(BF16), wor_ref[...]))
    xe = _rms(x1, g2_ref[...])
    row = lax.broadcasted_iota(jnp.int32, (tm + 2 * HALO, 1), 0)
    outside = ((row < HALO) & (i == 0)) | ((row >= tm + HALO) & (i == last))
    xe = jnp.where(outside, 0.0, xe).astype(BF16)

    def conv3(scr, cw_ref, cb_ref):
        return (scr[pl.ds(HALO - 1, tm), :] * cw_ref[0:1, :] + scr[pl.ds(HALO, tm), :] * cw_ref[1:2, :]
                + scr[pl.ds(HALO + 1, tm), :] * cw_ref[2:3, :] + cb_ref[...])

    ua_scr[...] = _dot(xe, wa_ref[...])
    ub_scr[...] = _dot(xe, wb_ref[...])
    gated = _gelu_tanh(conv3(ua_scr, cwa_ref, cba_ref)) * conv3(ub_scr, cwb_ref, cbb_ref)
    y = x1[HALO:HALO + tm] + _dot(gated.astype(BF16), wd_ref[...])
    o_ref[0] = _rms(y, gf_ref[...])


def _mix_ffn(x, zh, zr, hy_out_g, w_out, norm2_g, w_up, conv_w, conv_b, w_down, final_g):
    B, L, D = x.shape
    tm = min(ROW_TILE, L)
    blocks = tm // FHALO

    def halo3(w):
        prev = pl.BlockSpec((1, FHALO, w), lambda b, i: (b, jnp.maximum(i * blocks - 1, 0), 0))
        main = pl.BlockSpec((1, tm, w), lambda b, i: (b, i, 0))
        nxt = pl.BlockSpec((1, FHALO, w), lambda b, i: (b, jnp.minimum((i + 1) * blocks, L // FHALO - 1), 0))
        return [prev, main, nxt]

    consts = (hy_out_g.reshape(1, -1), w_out[:HY_W].astype(BF16), w_out[HY_W:].astype(BF16),
              norm2_g.reshape(1, D), w_up[:, :D_FF].astype(BF16), w_up[:, D_FF:].astype(BF16),
              conv_w[:, :D_FF], conv_w[:, D_FF:], conv_b[:D_FF].reshape(1, -1), conv_b[D_FF:].reshape(1, -1),
              w_down.astype(BF16), final_g.reshape(1, D))
    return pl.pallas_call(
        functools.partial(_ffn_kernel, tm=tm),
        grid=(B, L // tm),
        in_specs=halo3(D) + halo3(HY_W) + halo3(RT_W) + [_const_spec(a.shape) for a in consts],
        out_specs=pl.BlockSpec((1, tm, D), lambda b, i: (b, i, 0)),
        out_shape=jax.ShapeDtypeStruct((B, L, D), F32),
        scratch_shapes=[pltpu.VMEM((tm + 2 * HALO, D_FF), F32)] * 2,
        compiler_params=_params("parallel", "arbitrary"),
        name="mix_ffn",
    )(x, x, x, zh, zh, zh, zr, zr, zr, *consts)


def _trunk(x, prm, spectra, fwd, inv, rot, ret_tables, n):
    hy, rt = _in_proj(x, prm["norm1_g"], prm["w_in"], prm["hy_conv_w"], prm["hy_conv_b"], *rot)
    z = _long_conv(hy, *spectra, prm["hy_bias"], fwd, inv, n)
    zr = _retention(rt, ret_tables, prm["ret_gn_g"], prm["ret_gn_b"])
    return _mix_ffn(x, z, zr, prm["hy_out_g"], prm["w_out"], prm["norm2_g"], prm["w_up"], prm["ffn_conv_w"],
                    prm["ffn_conv_b"], prm["w_down"], prm["final_g"])


def _layer(xs, prm, n=CONV_BLOCKS):
    L = xs[0].shape[1]
    assert all(x.shape[1] == L for x in xs) and L % (n * CMAC_ROWS) == 0 and L % (RET_CHUNK * RET_UNROLL) == 0
    fwd, inv = _dft_tables(L // n)
    spectra = _filter_spectra(L, n, fwd, prm["filt_w1"], prm["filt_b1"], prm["filt_w2"], prm["filt_b2"],
                              prm["filt_w3"], prm["filt_b3"], prm["filt_freq"], prm["filt_w4"])
    rot = _rotary_tables(L)
    ret_tables = _retention_tables()
    return tuple(_trunk(x, prm, spectra, fwd, inv, rot, ret_tables, n) for x in xs)


def kernel(x_prompt, x_sample, norm1_g, w_in, hy_conv_w, hy_conv_b, filt_w1, filt_b1, filt_w2, filt_b2,
           filt_w3, filt_b3, filt_freq, filt_w4, hy_bias, hy_out_g, ret_gn_g, ret_gn_b, w_out, norm2_g,
           w_up, ffn_conv_w, ffn_conv_b, w_down, final_g):
    layer = dict(norm1_g=norm1_g, w_in=w_in, hy_conv_w=hy_conv_w, hy_conv_b=hy_conv_b, filt_w1=filt_w1,
                 filt_b1=filt_b1, filt_w2=filt_w2, filt_b2=filt_b2, filt_w3=filt_w3, filt_b3=filt_b3,
                 filt_freq=filt_freq, filt_w4=filt_w4, hy_bias=hy_bias, hy_out_g=hy_out_g,
                 ret_gn_g=ret_gn_g, ret_gn_b=ret_gn_b, w_out=w_out, norm2_g=norm2_g, w_up=w_up,
                 ffn_conv_w=ffn_conv_w, ffn_conv_b=ffn_conv_b, w_down=w_down)
    assert all(v.shape[0] == 1 for v in layer.values())
    prm = {k: v[0] for k, v in layer.items()}
    prm["final_g"] = final_g
    return _layer((x_prompt, x_sample), prm)
```

```python
import functools
import math

import jax
import jax.numpy as jnp
import numpy as np
from jax import lax
from jax.experimental import pallas as pl
from jax.experimental.pallas import tpu as pltpu

F32 = jnp.float32
BF16 = jnp.bfloat16

D_MODEL = 1024
HY_W = D_MODEL // 2
RT_W = D_MODEL - HY_W
HY_ORDER = 2
HY_COLS = (HY_ORDER + 1) * HY_W
RT_COLS = 4 * RT_W
FILTER_BANDS = 16
FILTER_HIDDEN = 64
DECAY_TARGET = 1e-2
FAST_DECAY_PCT = 0.3
SLOW_DECAY_PCT = 1.5
RT_HEADS = 8
RT_HEAD_DIM = RT_W // RT_HEADS
ROPE_BASE = 10000.0
DECAY_OFFSET_FWD = 5.0
DECAY_OFFSET_BWD = 5.5
D_FF = ((8 * D_MODEL // 3 + 127) // 128) * 128
EPS = 1e-6

LANES = 128
HALO = 8
FHALO = 16
DFT_SPLIT = 32
CONV_BLOCKS = 4
CONV_CT = 256
ROW_TILE = 512
IN_ROW_TILE = 1024
RET_CHUNK = 256
RET_UNROLL = 4
RET_LANES = 256
CMAC_ROWS = 128
VMEM_LIMIT = 58 * 1024 * 1024


def _params(*sem):
    return pltpu.CompilerParams(dimension_semantics=sem, vmem_limit_bytes=VMEM_LIMIT)


def _dot(a, b):
    return jnp.dot(a, b, preferred_element_type=F32)


def _dot_hi(a, b):
    return jnp.dot(a, b, preferred_element_type=F32, precision=lax.Precision.HIGHEST)


def _const_spec(shape):
    nd = len(shape)
    return pl.BlockSpec(shape, lambda *_: (0,) * nd, pipeline_mode=pl.Buffered(1))


def _dft_tables(P):
    k = jnp.arange(P, dtype=jnp.int32)[:, None]

    def cs(tt):
        ang = (((2 * k + 1) * tt) % (4 * P)).astype(F32) * (math.pi / (2 * P))
        return jnp.cos(ang), jnp.sin(ang)

    ca, sa = cs(DFT_SPLIT * jnp.arange(P // DFT_SPLIT, dtype=jnp.int32)[None, :])
    cb, sb = cs(jnp.arange(DFT_SPLIT, dtype=jnp.int32)[None, :])
    cos = (ca[:, :, None] * cb[:, None, :] - sa[:, :, None] * sb[:, None, :]).reshape(P, P)
    sin = (sa[:, :, None] * cb[:, None, :] + ca[:, :, None] * sb[:, None, :]).reshape(P, P)
    fwd = jnp.concatenate([cos, -sin], axis=0).astype(BF16)
    inv = jnp.concatenate([cos.T, -sin.T], axis=1).astype(BF16)
    return fwd, inv


def _rotary_tables(L):
    inv = ROPE_BASE ** (-np.arange(0, RT_HEAD_DIM, 2, dtype=np.float64) / RT_HEAD_DIM)
    ang = np.arange(L, dtype=np.float64)[:, None] * inv[None, :]
    cos, sin = np.cos(ang), np.sin(ang)
    reps = LANES // RT_HEAD_DIM
    cos_t = np.tile(np.concatenate([cos, cos], axis=1), (1, reps))
    sin_t = np.tile(np.concatenate([-sin, sin], axis=1), (1, reps))
    return jnp.asarray(cos_t, F32), jnp.asarray(sin_t, F32)


def _retention_tables():
    C = RET_CHUNK
    heads = np.arange(RT_HEADS, dtype=np.float64)
    log_f = np.log(1.0 - 2.0 ** (-DECAY_OFFSET_FWD - heads))
    log_b = np.log(1.0 - 2.0 ** (-DECAY_OFFSET_BWD - heads))
    idx = np.arange(C, dtype=np.float64)
    diff = idx[:, None] - idx[None, :]
    lower = diff >= 0
    d_f = np.where(lower[None], np.exp(np.where(lower, diff, 0.0)[None] * log_f[:, None, None]), 0.0)
    d_b = np.where(~lower[None], np.exp(np.where(~lower, -diff, 0.0)[None] * log_b[:, None, None]), 0.0)
    dmask = (d_f + d_b).reshape(RT_HEADS // 2, 2 * C, C)

    def lanes(tab):
        rows = tab.shape[1]
        t = np.repeat(tab[:, :, None], RT_HEAD_DIM, axis=2)
        t = t.reshape(RT_HEADS // 2, 2, rows, RT_HEAD_DIM).transpose(0, 2, 1, 3)
        return t.reshape(RT_HEADS // 2, rows, LANES)

    qd_f = lanes(np.exp((idx + 1.0)[None, :] * log_f[:, None]))
    kd_f = lanes(np.exp((C - 1.0 - idx)[None, :] * log_f[:, None]))
    qd_b = lanes(np.exp((C - idx)[None, :] * log_b[:, None]))
    kd_b = lanes(np.exp(idx[None, :] * log_b[:, None]))
    cd_f = lanes(np.exp(C * log_f)[:, None])
    cd_b = lanes(np.exp(C * log_b)[:, None])
    return tuple(jnp.asarray(t, F32) for t in (dmask, qd_f, kd_f, qd_b, kd_b, cd_f, cd_b))


def _filter_kernel(w1t_ref, w1c_ref, w1s_ref, b1_ref, w2_ref, b2_ref, w3_ref, b3_ref, fr_ref, w4_ref, band_ref,
                   delta_ref, fwd_ref, gpos_ref, gneg_ref, prev_scr, tap_scr, *, P, L):
    j = pl.program_id(0)
    pos_l = (j * P + lax.broadcasted_iota(jnp.int32, (1, P), 1)).astype(F32)
    ang = (2.0 * math.pi / L) * pos_l * band_ref[...]
    freq = fr_ref[...]
    h = (w1t_ref[...] * (pos_l * (1.0 / (L - 1))) + _dot_hi(w1c_ref[...], jnp.cos(ang))
         + _dot_hi(w1s_ref[...], -jnp.sin(ang)) + b1_ref[...])
    h = jnp.sin(freq * h)
    h = jnp.sin(freq * (_dot_hi(w2_ref[...], h) + b2_ref[...]))
    h = jnp.sin(freq * (_dot_hi(w3_ref[...], h) + b3_ref[...]))
    row = lax.broadcasted_iota(jnp.int32, (P, 1), 0)
    t = (j * P + row).astype(F32) * (1.0 / (L - 1))
    sgn = (1 - 2 * (row & 1)).astype(F32)
    lag0 = (row == 0) & (j == 0)
    tn = (((0,), (0,)), ((), ()))
    for c in range(gpos_ref.shape[2] // CONV_CT):
        cs = pl.ds(c * CONV_CT, CONV_CT)
        decay = jnp.exp(-t * delta_ref[:, cs])
        taps = []
        for d in range(2):
            tp = lax.dot_general(h, w4_ref[d, :, cs], tn, preferred_element_type=F32,
                                 precision=lax.Precision.HIGHEST) * decay
            if d == 1:
                tp = jnp.where(lag0, 0.0, tp)
            taps.append(tp.astype(BF16))
        sa = _dot(fwd_ref[...], taps[0])
        sb = _dot(fwd_ref[...], taps[1])

        @pl.when(j == 0)
        def _():
            gpos_ref[0, pl.ds(0, P), cs] = (sa[:P] + sb[:P]).astype(BF16)
            gpos_ref[0, pl.ds(P, P), cs] = (sa[P:] - sb[P:]).astype(BF16)
            gneg_ref[0, :, cs] = jnp.zeros((2 * P, CONV_CT), BF16)

        @pl.when(j > 0)
        def _():
            pa_re = prev_scr[0, pl.ds(0, P), cs] - tap_scr[0:1, cs]
            pa_im = prev_scr[0, pl.ds(P, P), cs]
            gpos_ref[0, pl.ds(0, P), cs] = (sa[:P] - sgn * pa_im).astype(BF16)
            gpos_ref[0, pl.ds(P, P), cs] = (sa[P:] + sgn * pa_re).astype(BF16)
            pb_re = prev_scr[1, pl.ds(0, P), cs] - tap_scr[1:2, cs]
            pb_im = -prev_scr[1, pl.ds(P, P), cs]
            gneg_ref[0, pl.ds(0, P), cs] = (sb[:P] + sgn * pb_im).astype(BF16)
            gneg_ref[0, pl.ds(P, P), cs] = (-sb[P:] - sgn * pb_re).astype(BF16)

        prev_scr[0, :, cs] = sa
        prev_scr[1, :, cs] = sb
        tap_scr[0:1, cs] = taps[0][0:1].astype(F32)
        tap_scr[1:2, cs] = taps[1][0:1].astype(F32)


def _filter_spectra(L, n, fwd, w1, b1, w2, b2, w3, b3, freq, w4):
    P = L // n
    cols = HY_ORDER * HY_W
    bands = jnp.asarray(np.linspace(1e-4, FILTER_BANDS - 1, FILTER_BANDS), F32)
    deltas = np.abs(np.linspace(math.log(DECAY_TARGET) / FAST_DECAY_PCT,
                                math.log(DECAY_TARGET) / SLOW_DECAY_PCT, HY_W))
    delta = jnp.asarray(np.tile(deltas, HY_ORDER)[None, :], F32)
    w4r = w4.reshape(FILTER_HIDDEN, 2, cols).transpose(1, 0, 2)
    col = lambda a: a.reshape(-1, 1)
    args = (col(w1[0]), w1[1:1 + FILTER_BANDS].T, w1[1 + FILTER_BANDS:].T, col(b1), w2.T, col(b2), w3.T, col(b3),
            col(freq), w4r, col(bands), delta, fwd)
    out = pl.BlockSpec((1, 2 * P, cols), lambda j: (j, 0, 0))
    return pl.pallas_call(
        functools.partial(_filter_kernel, P=P, L=L),
        grid=(n,),
        in_specs=[_const_spec(a.shape) for a in args],
        out_specs=[out, out],
        out_shape=[jax.ShapeDtypeStruct((n, 2 * P, cols), BF16)] * 2,
        scratch_shapes=[pltpu.VMEM((2, 2 * P, cols), F32), pltpu.VMEM((8, cols), F32)],
        compiler_params=_params("arbitrary"),
        name="hyena_filter",
    )(*args)


def _in_kernel(xp_ref, xm_ref, xn_ref, g1_ref, wh_ref, wr_ref, cw_ref, cb_ref, cos_ref, sin_ref,
               hy_ref, rt_ref, *, tm):
    i = pl.program_id(1)
    last = pl.num_programs(1) - 1
    xe = jnp.concatenate([xm_ref[0], xn_ref[0], xp_ref[0]], axis=0)
    xe = xe * lax.rsqrt(jnp.mean(xe * xe, axis=-1, keepdims=True) + EPS) * g1_ref[...]
    row = lax.broadcasted_iota(jnp.int32, (tm + 2 * HALO, 1), 0)
    outside = ((row >= tm + HALO) & (i == 0)) | ((row >= tm) & (row < tm + HALO) & (i == last))
    xe = jnp.where(outside, 0.0, xe).astype(BF16)
    pe = _dot(xe, wh_ref[...])
    hy = (pltpu.roll(pe, 1, 0)[:tm] * cw_ref[0:1, :] + pe[:tm] * cw_ref[1:2, :]
          + pltpu.roll(pe, tm + 2 * HALO - 1, 0)[:tm] * cw_ref[2:3, :] + cb_ref[...])
    hy_ref[0] = hy.astype(BF16)

    pr = _dot(xe[:tm], wr_ref[...])
    cos = cos_ref[...]
    sin = sin_ref[...]
    lane = lax.broadcasted_iota(jnp.int32, (1, LANES), 1)
    first_half = (lane & (RT_HEAD_DIM // 2)) == 0
    for j in range(2 * RT_W // LANES):
        xg = pr[:, j * LANES:(j + 1) * LANES]
        partner = jnp.where(first_half, pltpu.roll(xg, LANES - RT_HEAD_DIM // 2, 1),
                            pltpu.roll(xg, RT_HEAD_DIM // 2, 1))
        r = xg * cos + partner * sin
        if j >= RT_W // LANES:
            r = r * RT_HEAD_DIM ** -0.5
        rt_ref[0, :, j * LANES:(j + 1) * LANES] = r.astype(BF16)
    rt_ref[0, :, 2 * RT_W:] = pr[:, 2 * RT_W:].astype(BF16)


def _halo_specs(tm, L, D):
    blocks = tm // HALO
    prev = pl.BlockSpec((1, HALO, D), lambda b, i: (b, jnp.maximum(i * blocks - 1, 0), 0))
    main = pl.BlockSpec((1, tm, D), lambda b, i: (b, i, 0))
    nxt = pl.BlockSpec((1, HALO, D), lambda b, i: (b, jnp.minimum((i + 1) * blocks, L // HALO - 1), 0))
    return [prev, main, nxt]


def _in_proj(x, g1, w_in, cw, cb, cos_t, sin_t):
    B, L, D = x.shape
    tm = min(IN_ROW_TILE, L)
    wh = w_in[:, :HY_COLS].astype(BF16)
    wr = w_in[:, HY_COLS:].astype(BF16)
    consts = (g1.reshape(1, D), wh, wr, cw, cb.reshape(1, -1))
    return pl.pallas_call(
        functools.partial(_in_kernel, tm=tm),
        grid=(B, L // tm),
        in_specs=_halo_specs(tm, L, D) + [_const_spec(a.shape) for a in consts]
        + [pl.BlockSpec((tm, LANES), lambda b, i: (i, 0))] * 2,
        out_specs=[pl.BlockSpec((1, tm, HY_COLS), lambda b, i: (b, i, 0)),
                   pl.BlockSpec((1, tm, RT_COLS), lambda b, i: (b, i, 0))],
        out_shape=[jax.ShapeDtypeStruct((B, L, HY_COLS), BF16),
                   jax.ShapeDtypeStruct((B, L, RT_COLS), BF16)],
        compiler_params=_params("parallel", "arbitrary"),
        name="in_proj",
    )(x, x, x, *consts, cos_t, sin_t)


def _conv_kernel(u_ref, x1_ref, x2_ref, gp0_ref, gn0_ref, gp1_ref, gn1_ref, bias_ref, fwd_ref, inv_ref, o_ref,
                 spec_scr, prod_scr, z_scr, *, P, n):
    def order(read_u, gate_ref, gpos_ref, gneg_ref, bias, write):
        for j in range(n):
            spec_scr[j] = _dot(fwd_ref[...], read_u(pl.ds(j * P, P))).astype(BF16)
        for i in range(n):
            for t in range(P // CMAC_ROWS):
                r_re = pl.ds(t * CMAC_ROWS, CMAC_ROWS)
                r_im = pl.ds(P + t * CMAC_ROWS, CMAC_ROWS)
                acc_re = jnp.zeros((CMAC_ROWS, CONV_CT), BF16)
                acc_im = jnp.zeros((CMAC_ROWS, CONV_CT), BF16)
                for j in range(n):
                    g_ref, d = (gpos_ref, i - j) if i >= j else (gneg_ref, j - i)
                    g_re, g_im = g_ref[d, r_re, :], g_ref[d, r_im, :]
                    u_re, u_im = spec_scr[j, r_re, :], spec_scr[j, r_im, :]
                    acc_re += g_re * u_re - g_im * u_im
                    acc_im += g_re * u_im + g_im * u_re
                prod_scr[i, r_re, :] = acc_re
                prod_scr[i, r_im, :] = acc_im
        for i in range(n):
            y = _dot(inv_ref[...], prod_scr[i]) * (1.0 / P)
            blk = pl.ds(i * P, P)
            u = read_u(blk).astype(F32)
            write(blk, (gate_ref[0, blk, :].astype(F32) * (y + u * bias)).astype(BF16))

    def to_scratch(blk, val):
        z_scr[blk, :] = val

    def to_output(blk, val):
        o_ref[0, blk, :] = val

    order(lambda blk: u_ref[0, blk, :], x1_ref, gp0_ref, gn0_ref, bias_ref[0:1, :], to_scratch)
    order(lambda blk: z_scr[blk, :], x2_ref, gp1_ref, gn1_ref, bias_ref[1:2, :], to_output)


def _long_conv(hy, gpos, gneg, bias, fwd, inv, n):
    B, L = hy.shape[:2]
    P = L // n
    nct = HY_W // CONV_CT
    col = lambda which: pl.BlockSpec((1, L, CONV_CT), lambda c, b: (b, 0, which * nct + c))
    spectra = lambda o: pl.BlockSpec((n, 2 * P, CONV_CT), lambda c, b: (0, 0, o * nct + c),
                                     pipeline_mode=pl.Buffered(1))
    return pl.pallas_call(
        functools.partial(_conv_kernel, P=P, n=n),
        grid=(nct, B),
        in_specs=[col(0), col(1), col(2), spectra(0), spectra(0), spectra(1), spectra(1),
                  pl.BlockSpec((HY_ORDER, CONV_CT), lambda c, b: (0, c)),
                  _const_spec(fwd.shape), _const_spec(inv.shape)],
        out_specs=pl.BlockSpec((1, L, CONV_CT), lambda c, b: (b, 0, c)),
        out_shape=jax.ShapeDtypeStruct((B, L, HY_W), BF16),
        scratch_shapes=[pltpu.VMEM((n, 2 * P, CONV_CT), BF16)] * 2 + [pltpu.VMEM((L, CONV_CT), BF16)],
        compiler_params=_params("arbitrary", "arbitrary"),
        name="long_conv",
    )(hy, hy, hy, gpos, gneg, gpos, gneg, bias, fwd, inv)


def _ret_kernel(q_ref, k_ref, v_ref, g_ref, dm_ref, qdf_ref, kdf_ref, qdb_ref, kdb_ref, cdf_ref, cdb_ref,
                avg_ref, gng_ref, gnb_ref, o_ref, acc_scr, kv_scr, sb_scr, *, L):
    C = RET_CHUNK
    pairs = RET_LANES // LANES
    nchunk = L // C
    lane = lax.broadcasted_iota(jnp.int32, (1, LANES), 1)
    head_a = lane < RT_HEAD_DIM
    ri = lax.broadcasted_iota(jnp.int32, (LANES, LANES), 0)
    ci = lax.broadcasted_iota(jnp.int32, (LANES, LANES), 1)
    same_head = ((ri < RT_HEAD_DIM) == (ci < RT_HEAD_DIM)).astype(F32)
    zero = jnp.zeros((), BF16)

    def split_heads(a):
        return jnp.concatenate([jnp.where(head_a, a, zero), jnp.where(head_a, zero, a)], axis=0)

    def kv_body(nidx, carry):
        rows = pl.ds(pl.multiple_of(nidx * C, C), C)
        for p in range(pairs):
            ls = slice(p * LANES, (p + 1) * LANES)
            kc, vc = k_ref[0, rows, ls].astype(F32), v_ref[0, rows, ls]
            kt = jnp.concatenate([(kc * kdf_ref[p]).T, (kc * kdb_ref[p]).T], axis=0).astype(BF16)
            kv_scr[nidx, p] = _dot(kt, vc) * jnp.concatenate([same_head, same_head], axis=0)
        return carry

    def bwd_scan(it, states):
        nidx = nchunk - 1 - it
        new = []
        for p in range(pairs):
            sb_scr[nidx, p] = states[p].astype(BF16)
            new.append(states[p] * cdb_ref[p] + kv_scr[nidx, p, pl.ds(LANES, LANES), :])
        return tuple(new)

    def main_body(nidx, states):
        rows = pl.ds(pl.multiple_of(nidx * C, C), C)
        new = []
        for p in range(pairs):
            ls = slice(p * LANES, (p + 1) * LANES)
            qc, kc, vc = q_ref[0, rows, ls], k_ref[0, rows, ls], v_ref[0, rows, ls]
            s = lax.dot_general(split_heads(qc), kc, (((1,), (1,)), ((), ())), preferred_element_type=F32)
            s = (s * dm_ref[p]).astype(BF16)
            qf = (qc.astype(F32) * qdf_ref[p]).astype(BF16)
            qb = (qc.astype(F32) * qdb_ref[p]).astype(BF16)
            lhs = jnp.concatenate([s[:C], s[C:], qf, qb], axis=1)
            rhs = jnp.concatenate([split_heads(vc), states[p].astype(BF16), sb_scr[nidx, p]], axis=0)
            acc_scr[rows, ls] = _dot(lhs, rhs)
            new.append(states[p] * cdf_ref[p] + kv_scr[nidx, p, pl.ds(0, LANES), :])
        return tuple(new)

    init = tuple(jnp.zeros((LANES, LANES), F32) for _ in range(pairs))
    lax.fori_loop(0, nchunk, kv_body, 0, unroll=RET_UNROLL)
    lax.fori_loop(0, nchunk, bwd_scan, init)
    lax.fori_loop(0, nchunk, main_body, init, unroll=RET_UNROLL)

    tn = min(ROW_TILE, L)
    avg = avg_ref[...]

    def group_mean(a):
        hi = a.astype(BF16)
        lo = (a - hi.astype(F32)).astype(BF16)
        return _dot(hi, avg) + _dot(lo, avg)

    def norm_body(it, carry):
        rows = pl.ds(pl.multiple_of(it * tn, tn), tn)
        o = acc_scr[rows, :]
        oc = o - group_mean(o)
        var = _dot((oc * oc).astype(BF16), avg)
        on = oc * lax.rsqrt(var + EPS) * gng_ref[...] + gnb_ref[...]
        g = g_ref[0, rows, :].astype(F32)
        o_ref[0, rows, :] = (on * g * (0.5 + 0.5 * jnp.tanh(0.5 * g))).astype(BF16)
        return carry

    lax.fori_loop(0, L // tn, norm_body, 0, unroll=2)


def _retention(rt, tables, gn_g, gn_b):
    B, L, _ = rt.shape
    groups = RT_W // RET_LANES
    pairs = RET_LANES // LANES
    li = jnp.arange(RET_LANES) // RT_HEAD_DIM
    avg = ((li[:, None] == li[None, :]).astype(F32) / RT_HEAD_DIM).astype(BF16)
    col = lambda which: pl.BlockSpec((1, L, RET_LANES), lambda g, b: (b, 0, which * groups + g))
    tab = lambda a: pl.BlockSpec((pairs,) + a.shape[1:], lambda g, b: (g, 0, 0))
    vec = pl.BlockSpec((1, RET_LANES), lambda g, b: (0, g))
    return pl.pallas_call(
        functools.partial(_ret_kernel, L=L),
        grid=(groups, B),
        in_specs=[col(0), col(1), col(2), col(3)] + [tab(a) for a in tables] + [_const_spec(avg.shape), vec, vec],
        out_specs=pl.BlockSpec((1, L, RET_LANES), lambda g, b: (b, 0, g)),
        out_shape=jax.ShapeDtypeStruct((B, L, RT_W), BF16),
        scratch_shapes=[pltpu.VMEM((L, RET_LANES), F32),
                        pltpu.VMEM((L // RET_CHUNK, pairs, 2 * LANES, LANES), F32),
                        pltpu.VMEM((L // RET_CHUNK, pairs, LANES, LANES), BF16)],
        compiler_params=_params("arbitrary", "arbitrary"),
        name="retention",
    )(rt, rt, rt, rt, *tables, avg, gn_g.reshape(1, -1), gn_b.reshape(1, -1))


def _gelu_tanh(a):
    return 0.5 * a * (1.0 + jnp.tanh(math.sqrt(2.0 / math.pi) * (a + 0.044715 * (a * a * a))))


def _rms(x, g):
    return x * lax.rsqrt(jnp.mean(x * x, axis=-1, keepdims=True) + EPS) * g


def _ffn_kernel(xp_ref, xm_ref, xn_ref, hp_ref, hm_ref, hn_ref, rp_ref, rm_ref, rn_ref, hg_ref, woh_ref, wor_ref,
                g2_ref, wa_ref, wb_ref, cwa_ref, cwb_ref, cba_ref, cbb_ref, wd_ref, gf_ref, o_ref,
                ua_scr, ub_scr, *, tm):
    i = pl.program_id(1)
    last = pl.num_programs(1) - 1
    def cat(p, m, n):
        return jnp.concatenate([p[0].astype(F32)[FHALO - HALO:], m[0].astype(F32), n[0].astype(F32)[:HALO]], axis=0)

    yh = _rms(cat(hp_ref, hm_ref, hn_ref), hg_ref[...]).astype(BF16)
    x1 = (cat(xp_ref, xm_ref, xn_ref) + _dot(yh, woh_ref[...])
          + _dot(cat(rp_ref, rm_ref, rn_ref).astype(BF16), wor_ref[...]))
    xe = _rms(x1, g2_ref[...])
    row = lax.broadcasted_iota(jnp.int32, (tm + 2 * HALO, 1), 0)
    outside = ((row < HALO) & (i == 0)) | ((row >= tm + HALO) & (i == last))
    xe = jnp.where(outside, 0.0, xe).astype(BF16)

    def conv3(scr, cw_ref, cb_ref):
        return (scr[pl.ds(HALO - 1, tm), :] * cw_ref[0:1, :] + scr[pl.ds(HALO, tm), :] * cw_ref[1:2, :]
                + scr[pl.ds(HALO + 1, tm), :] * cw_ref[2:3, :] + cb_ref[...])

    ua_scr[...] = _dot(xe, wa_ref[...])
    ub_scr[...] = _dot(xe, wb_ref[...])
    gated = _gelu_tanh(conv3(ua_scr, cwa_ref, cba_ref)) * conv3(ub_scr, cwb_ref, cbb_ref)
    y = x1[HALO:HALO + tm] + _dot(gated.astype(BF16), wd_ref[...])
    o_ref[0] = _rms(y, gf_ref[...])


def _mix_ffn(x, zh, zr, hy_out_g, w_out, norm2_g, w_up, conv_w, conv_b, w_down, final_g):
    B, L, D = x.shape
    tm = min(ROW_TILE, L)
    blocks = tm // FHALO

    def halo3(w):
        prev = pl.BlockSpec((1, FHALO, w), lambda b, i: (b, jnp.maximum(i * blocks - 1, 0), 0))
        main = pl.BlockSpec((1, tm, w), lambda b, i: (b, i, 0))
        nxt = pl.BlockSpec((1, FHALO, w), lambda b, i: (b, jnp.minimum((i + 1) * blocks, L // FHALO - 1), 0))
        return [prev, main, nxt]

    consts = (hy_out_g.reshape(1, -1), w_out[:HY_W].astype(BF16), w_out[HY_W:].astype(BF16),
              norm2_g.reshape(1, D), w_up[:, :D_FF].astype(BF16), w_up[:, D_FF:].astype(BF16),
              conv_w[:, :D_FF], conv_w[:, D_FF:], conv_b[:D_FF].reshape(1, -1), conv_b[D_FF:].reshape(1, -1),
              w_down.astype(BF16), final_g.reshape(1, D))
    return pl.pallas_call(
        functools.partial(_ffn_kernel, tm=tm),
        grid=(B, L // tm),
        in_specs=halo3(D) + halo3(HY_W) + halo3(RT_W) + [_const_spec(a.shape) for a in consts],
        out_specs=pl.BlockSpec((1, tm, D), lambda b, i: (b, i, 0)),
        out_shape=jax.ShapeDtypeStruct((B, L, D), F32),
        scratch_shapes=[pltpu.VMEM((tm + 2 * HALO, D_FF), F32)] * 2,
        compiler_params=_params("parallel", "arbitrary"),
        name="mix_ffn",
    )(x, x, x, zh, zh, zh, zr, zr, zr, *consts)


def _trunk(x, prm, spectra, fwd, inv, rot, ret_tables, n):
    hy, rt = _in_proj(x, prm["norm1_g"], prm["w_in"], prm["hy_conv_w"], prm["hy_conv_b"], *rot)
    z = _long_conv(hy, *spectra, prm["hy_bias"], fwd, inv, n)
    zr = _retention(rt, ret_tables, prm["ret_gn_g"], prm["ret_gn_b"])
    return _mix_ffn(x, z, zr, prm["hy_out_g"], prm["w_out"], prm["norm2_g"], prm["w_up"], prm["ffn_conv_w"],
                    prm["ffn_conv_b"], prm["w_down"], prm["final_g"])


def _layer(xs, prm, n=CONV_BLOCKS):
    L = xs[0].shape[1]
    assert all(x.shape[1] == L for x in xs) and L % (n * CMAC_ROWS) == 0 and L % (RET_CHUNK * RET_UNROLL) == 0
    fwd, inv = _dft_tables(L // n)
    spectra = _filter_spectra(L, n, fwd, prm["filt_w1"], prm["filt_b1"], prm["filt_w2"], prm["filt_b2"],
                              prm["filt_w3"], prm["filt_b3"], prm["filt_freq"], prm["filt_w4"])
    rot = _rotary_tables(L)
    ret_tables = _retention_tables()
    return tuple(_trunk(x, prm, spectra, fwd, inv, rot, ret_tables, n) for x in xs)


def kernel(x_prompt, x_sample, norm1_g, w_in, hy_conv_w, hy_conv_b, filt_w1, filt_b1, filt_w2, filt_b2,
           filt_w3, filt_b3, filt_freq, filt_w4, hy_bias, hy_out_g, ret_gn_g, ret_gn_b, w_out, norm2_g,
           w_up, ffn_conv_w, ffn_conv_b, w_down, final_g):
    layer = dict(norm1_g=norm1_g, w_in=w_in, hy_conv_w=hy_conv_w, hy_conv_b=hy_conv_b, filt_w1=filt_w1,
                 filt_b1=filt_b1, filt_w2=filt_w2, filt_b2=filt_b2, filt_w3=filt_w3, filt_b3=filt_b3,
                 filt_freq=filt_freq, filt_w4=filt_w4, hy_bias=hy_bias, hy_out_g=hy_out_g,
                 ret_gn_g=ret_gn_g, ret_gn_b=ret_gn_b, w_out=w_out, norm2_g=norm2_g, w_up=w_up,
                 ffn_conv_w=ffn_conv_w, ffn_conv_b=ffn_conv_b, w_down=w_down)
    assert all(v.shape[0] == 1 for v in layer.values())
    prm = {k: v[0] for k, v in layer.items()}
    prm["final_g"] = final_g
    return _layer((x_prompt, x_sample), prm)
```

```python
import functools
import math

import jax
import jax.numpy as jnp
import numpy as np
from jax import lax
from jax.experimental import pallas as pl
from jax.experimental.pallas import tpu as pltpu

F32 = jnp.float32
BF16 = jnp.bfloat16

D_MODEL = 1024
HY_W = D_MODEL // 2
RT_W = D_MODEL - HY_W
HY_ORDER = 2
HY_COLS = (HY_ORDER + 1) * HY_W
RT_COLS = 4 * RT_W
FILTER_BANDS = 16
FILTER_HIDDEN = 64
DECAY_TARGET = 1e-2
FAST_DECAY_PCT = 0.3
SLOW_DECAY_PCT = 1.5
RT_HEADS = 8
RT_HEAD_DIM = RT_W // RT_HEADS
ROPE_BASE = 10000.0
DECAY_OFFSET_FWD = 5.0
DECAY_OFFSET_BWD = 5.5
D_FF = ((8 * D_MODEL // 3 + 127) // 128) * 128
EPS = 1e-6

LANES = 128
HALO = 8
FHALO = 16
DFT_SPLIT = 32
CONV_BLOCKS = 4
CONV_CT = 256
ROW_TILE = 512
IN_ROW_TILE = 1024
RET_CHUNK = 256
RET_UNROLL = 4
RET_LANES = 256
CMAC_ROWS = 128
VMEM_LIMIT = 58 * 1024 * 1024


def _params(*sem):
    return pltpu.CompilerParams(dimension_semantics=sem, vmem_limit_bytes=VMEM_LIMIT)


def _dot(a, b):
    return jnp.dot(a, b, preferred_element_type=F32)


def _dot_hi(a, b):
    return jnp.dot(a, b, preferred_element_type=F32, precision=lax.Precision.HIGHEST)


def _const_spec(shape):
    nd = len(shape)
    return pl.BlockSpec(shape, lambda *_: (0,) * nd, pipeline_mode=pl.Buffered(1))


def _dft_tables(P):
    k = jnp.arange(P, dtype=jnp.int32)[:, None]

    def cs(tt):
        ang = (((2 * k + 1) * tt) % (4 * P)).astype(F32) * (math.pi / (2 * P))
        return jnp.cos(ang), jnp.sin(ang)

    ca, sa = cs(DFT_SPLIT * jnp.arange(P // DFT_SPLIT, dtype=jnp.int32)[None, :])
    cb, sb = cs(jnp.arange(DFT_SPLIT, dtype=jnp.int32)[None, :])
    cos = (ca[:, :, None] * cb[:, None, :] - sa[:, :, None] * sb[:, None, :]).reshape(P, P)
    sin = (sa[:, :, None] * cb[:, None, :] + ca[:, :, None] * sb[:, None, :]).reshape(P, P)
    fwd = jnp.concatenate([cos, -sin], axis=0).astype(BF16)
    inv = jnp.concatenate([cos.T, -sin.T], axis=1).astype(BF16)
    return fwd, inv


def _rotary_tables(L):
    inv = ROPE_BASE ** (-np.arange(0, RT_HEAD_DIM, 2, dtype=np.float64) / RT_HEAD_DIM)
    ang = np.arange(L, dtype=np.float64)[:, None] * inv[None, :]
    cos, sin = np.cos(ang), np.sin(ang)
    reps = LANES // RT_HEAD_DIM
    cos_t = np.tile(np.concatenate([cos, cos], axis=1), (1, reps))
    sin_t = np.tile(np.concatenate([-sin, sin], axis=1), (1, reps))
    return jnp.asarray(cos_t, F32), jnp.asarray(sin_t, F32)


def _retention_tables():
    C = RET_CHUNK
    heads = np.arange(RT_HEADS, dtype=np.float64)
    log_f = np.log(1.0 - 2.0 ** (-DECAY_OFFSET_FWD - heads))
    log_b = np.log(1.0 - 2.0 ** (-DECAY_OFFSET_BWD - heads))
    idx = np.arange(C, dtype=np.float64)
    diff = idx[:, None] - idx[None, :]
    lower = diff >= 0
    d_f = np.where(lower[None], np.exp(np.where(lower, diff, 0.0)[None] * log_f[:, None, None]), 0.0)
    d_b = np.where(~lower[None], np.exp(np.where(~lower, -diff, 0.0)[None] * log_b[:, None, None]), 0.0)
    dmask = (d_f + d_b).reshape(RT_HEADS // 2, 2 * C, C)

    def lanes(tab):
        rows = tab.shape[1]
        t = np.repeat(tab[:, :, None], RT_HEAD_DIM, axis=2)
        t = t.reshape(RT_HEADS // 2, 2, rows, RT_HEAD_DIM).transpose(0, 2, 1, 3)
        return t.reshape(RT_HEADS // 2, rows, LANES)

    qd_f = lanes(np.exp((idx + 1.0)[None, :] * log_f[:, None]))
    kd_f = lanes(np.exp((C - 1.0 - idx)[None, :] * log_f[:, None]))
    qd_b = lanes(np.exp((C - idx)[None, :] * log_b[:, None]))
    kd_b = lanes(np.exp(idx[None, :] * log_b[:, None]))
    cd_f = lanes(np.exp(C * log_f)[:, None])
    cd_b = lanes(np.exp(C * log_b)[:, None])
    return tuple(jnp.asarray(t, F32) for t in (dmask, qd_f, kd_f, qd_b, kd_b, cd_f, cd_b))


def _filter_kernel(w1t_ref, w1c_ref, w1s_ref, b1_ref, w2_ref, b2_ref, w3_ref, b3_ref, fr_ref, w4_ref, band_ref,
                   delta_ref, fwd_ref, gpos_ref, gneg_ref, prev_scr, tap_scr, *, P, L):
    j = pl.program_id(0)

    @pl.when(j == 0)
    def _():
        prev_scr[...] = jnp.zeros(prev_scr.shape, F32)
        tap_scr[...] = jnp.zeros(tap_scr.shape, F32)

    pos_l = (j * P + lax.broadcasted_iota(jnp.int32, (1, P), 1)).astype(F32)
    ang = (2.0 * math.pi / L) * pos_l * band_ref[...]
    freq = fr_ref[...]
    h = (w1t_ref[...] * (pos_l * (1.0 / (L - 1))) + _dot_hi(w1c_ref[...], jnp.cos(ang))
         + _dot_hi(w1s_ref[...], -jnp.sin(ang)) + b1_ref[...])
    h = jnp.sin(freq * h)
    h = jnp.sin(freq * (_dot_hi(w2_ref[...], h) + b2_ref[...]))
    h = jnp.sin(freq * (_dot_hi(w3_ref[...], h) + b3_ref[...]))
    h_hi = h.astype(BF16)
    h_lo = (h - h_hi.astype(F32)).astype(BF16)
    row = lax.broadcasted_iota(jnp.int32, (P, 1), 0)
    t = (j * P + row).astype(F32) * (1.0 / (L - 1))
    sgn = (1 - 2 * (row & 1)).astype(F32)
    lag0 = (row == 0) & (j == 0)
    tn = (((0,), (0,)), ((), ()))
    for c in range(gpos_ref.shape[2] // CONV_CT):
        cs = pl.ds(c * CONV_CT, CONV_CT)
        decay = jnp.exp(-t * delta_ref[:, cs])
        taps = []
        for d in range(2):
            w = w4_ref[d, :, cs]
            w_hi = w.astype(BF16)
            w_lo = (w - w_hi.astype(F32)).astype(BF16)
            tp = (lax.dot_general(h_hi, w_hi, tn, preferred_element_type=F32)
                  + lax.dot_general(h_hi, w_lo, tn, preferred_element_type=F32)
                  + lax.dot_general(h_lo, w_hi, tn, preferred_element_type=F32)) * decay
            if d == 1:
                tp = jnp.where(lag0, 0.0, tp)
            taps.append(tp.astype(BF16))
        sa = _dot(fwd_ref[...], taps[0])
        sb = _dot(fwd_ref[...], taps[1])

        pa_re = prev_scr[0, pl.ds(0, P), cs] - tap_scr[0:1, cs]
        pa_im = prev_scr[0, pl.ds(P, P), cs]
        pb_re = prev_scr[1, pl.ds(0, P), cs] - tap_scr[1:2, cs]
        pb_im = -prev_scr[1, pl.ds(P, P), cs]
        first = j == 0
        gpos_ref[0, pl.ds(0, P), cs] = (sa[:P] + jnp.where(first, sb[:P], -sgn * pa_im)).astype(BF16)
        gpos_ref[0, pl.ds(P, P), cs] = (sa[P:] + jnp.where(first, -sb[P:], sgn * pa_re)).astype(BF16)
        gneg_ref[0, pl.ds(0, P), cs] = jnp.where(first, 0.0, sb[:P] + sgn * pb_im).astype(BF16)
        gneg_ref[0, pl.ds(P, P), cs] = jnp.where(first, 0.0, -sb[P:] - sgn * pb_re).astype(BF16)

        prev_scr[0, :, cs] = sa
        prev_scr[1, :, cs] = sb
        tap_scr[0:1, cs] = taps[0][0:1].astype(F32)
        tap_scr[1:2, cs] = taps[1][0:1].astype(F32)


def _filter_spectra(L, n, fwd, w1, b1, w2, b2, w3, b3, freq, w4):
    P = L // n
    cols = HY_ORDER * HY_W
    bands = jnp.asarray(np.linspace(1e-4, FILTER_BANDS - 1, FILTER_BANDS), F32)
    deltas = np.abs(np.linspace(math.log(DECAY_TARGET) / FAST_DECAY_PCT,
                                math.log(DECAY_TARGET) / SLOW_DECAY_PCT, HY_W))
    delta = jnp.asarray(np.tile(deltas, HY_ORDER)[None, :], F32)
    w4r = w4.reshape(FILTER_HIDDEN, 2, cols).transpose(1, 0, 2)
    col = lambda a: a.reshape(-1, 1)
    args = (col(w1[0]), w1[1:1 + FILTER_BANDS].T, w1[1 + FILTER_BANDS:].T, col(b1), w2.T, col(b2), w3.T, col(b3),
            col(freq), w4r, col(bands), delta, fwd)
    out = pl.BlockSpec((1, 2 * P, cols), lambda j: (j, 0, 0))
    return pl.pallas_call(
        functools.partial(_filter_kernel, P=P, L=L),
        grid=(n,),
        in_specs=[_const_spec(a.shape) for a in args],
        out_specs=[out, out],
        out_shape=[jax.ShapeDtypeStruct((n, 2 * P, cols), BF16)] * 2,
        scratch_shapes=[pltpu.VMEM((2, 2 * P, cols), F32), pltpu.VMEM((8, cols), F32)],
        compiler_params=_params("arbitrary"),
        name="hyena_filter",
    )(*args)


def _in_kernel(xp_ref, xm_ref, xn_ref, g1_ref, wh_ref, wr_ref, cw_ref, cb_ref, cos_ref, sin_ref,
               hy_ref, rt_ref, *, tm):
    i = pl.program_id(1)
    last = pl.num_programs(1) - 1
    xe = jnp.concatenate([xm_ref[0], xn_ref[0], xp_ref[0]], axis=0)
    xe = xe * lax.rsqrt(jnp.mean(xe * xe, axis=-1, keepdims=True) + EPS) * g1_ref[...]
    row = lax.broadcasted_iota(jnp.int32, (tm + 2 * HALO, 1), 0)
    outside = ((row >= tm + HALO) & (i == 0)) | ((row >= tm) & (row < tm + HALO) & (i == last))
    xe = jnp.where(outside, 0.0, xe).astype(BF16)
    pe = _dot(xe, wh_ref[...])
    hy = (pltpu.roll(pe, 1, 0)[:tm] * cw_ref[0:1, :] + pe[:tm] * cw_ref[1:2, :]
          + pltpu.roll(pe, tm + 2 * HALO - 1, 0)[:tm] * cw_ref[2:3, :] + cb_ref[...])
    hy_ref[0] = hy.astype(BF16)

    pr = _dot(xe[:tm], wr_ref[...])
    cos = cos_ref[...]
    sin = sin_ref[...]
    lane = lax.broadcasted_iota(jnp.int32, (1, LANES), 1)
    first_half = (lane & (RT_HEAD_DIM // 2)) == 0
    for j in range(2 * RT_W // LANES):
        xg = pr[:, j * LANES:(j + 1) * LANES]
        partner = jnp.where(first_half, pltpu.roll(xg, LANES - RT_HEAD_DIM // 2, 1),
                            pltpu.roll(xg, RT_HEAD_DIM // 2, 1))
        r = xg * cos + partner * sin
        if j >= RT_W // LANES:
            r = r * RT_HEAD_DIM ** -0.5
        rt_ref[0, :, j * LANES:(j + 1) * LANES] = r.astype(BF16)
    rt_ref[0, :, 2 * RT_W:] = pr[:, 2 * RT_W:].astype(BF16)


def _halo_specs(tm, L, D):
    blocks = tm // HALO
    prev = pl.BlockSpec((1, HALO, D), lambda b, i: (b, jnp.maximum(i * blocks - 1, 0), 0))
    main = pl.BlockSpec((1, tm, D), lambda b, i: (b, i, 0))
    nxt = pl.BlockSpec((1, HALO, D), lambda b, i: (b, jnp.minimum((i + 1) * blocks, L // HALO - 1), 0))
    return [prev, main, nxt]


def _in_proj(x, g1, w_in, cw, cb, cos_t, sin_t):
    B, L, D = x.shape
    tm = min(IN_ROW_TILE, L)
    wh = w_in[:, :HY_COLS].astype(BF16)
    wr = w_in[:, HY_COLS:].astype(BF16)
    consts = (g1.reshape(1, D), wh, wr, cw, cb.reshape(1, -1))
    return pl.pallas_call(
        functools.partial(_in_kernel, tm=tm),
        grid=(B, L // tm),
        in_specs=_halo_specs(tm, L, D) + [_const_spec(a.shape) for a in consts]
        + [pl.BlockSpec((tm, LANES), lambda b, i: (i, 0))] * 2,
        out_specs=[pl.BlockSpec((1, tm, HY_COLS), lambda b, i: (b, i, 0)),
                   pl.BlockSpec((1, tm, RT_COLS), lambda b, i: (b, i, 0))],
        out_shape=[jax.ShapeDtypeStruct((B, L, HY_COLS), BF16),
                   jax.ShapeDtypeStruct((B, L, RT_COLS), BF16)],
        compiler_params=_params("parallel", "arbitrary"),
        name="in_proj",
    )(x, x, x, *consts, cos_t, sin_t)


def _conv_kernel(u_ref, x1_ref, x2_ref, gp0_ref, gn0_ref, gp1_ref, gn1_ref, bias_ref, fwd_ref, inv_ref, o_ref,
                 spec_scr, prod_scr, z_scr, *, P, n):
    def order(read_u, gate_ref, gpos_ref, gneg_ref, bias, write):
        for j in range(n):
            spec_scr[j] = _dot(fwd_ref[...], read_u(pl.ds(j * P, P))).astype(BF16)
        for i in range(n):
            for t in range(P // CMAC_ROWS):
                r_re = pl.ds(t * CMAC_ROWS, CMAC_ROWS)
                r_im = pl.ds(P + t * CMAC_ROWS, CMAC_ROWS)
                acc_re = jnp.zeros((CMAC_ROWS, CONV_CT), BF16)
                acc_im = jnp.zeros((CMAC_ROWS, CONV_CT), BF16)
                for j in range(n):
                    g_ref, d = (gpos_ref, i - j) if i >= j else (gneg_ref, j - i)
                    g_re, g_im = g_ref[d, r_re, :], g_ref[d, r_im, :]
                    u_re, u_im = spec_scr[j, r_re, :], spec_scr[j, r_im, :]
                    acc_re += g_re * u_re - g_im * u_im
                    acc_im += g_re * u_im + g_im * u_re
                prod_scr[i, r_re, :] = acc_re
                prod_scr[i, r_im, :] = acc_im
        for i in range(n):
            y = _dot(inv_ref[...], prod_scr[i]) * (1.0 / P)
            blk = pl.ds(i * P, P)
            u = read_u(blk).astype(F32)
            write(blk, (gate_ref[0, blk, :].astype(F32) * (y + u * bias)).astype(BF16))

    def to_scratch(blk, val):
        z_scr[blk, :] = val

    def to_output(blk, val):
        o_ref[0, blk, :] = val

    order(lambda blk: u_ref[0, blk, :], x1_ref, gp0_ref, gn0_ref, bias_ref[0:1, :], to_scratch)
    order(lambda blk: z_scr[blk, :], x2_ref, gp1_ref, gn1_ref, bias_ref[1:2, :], to_output)


def _long_conv(hy, gpos, gneg, bias, fwd, inv, n):
    B, L = hy.shape[:2]
    P = L // n
    nct = HY_W // CONV_CT
    col = lambda which: pl.BlockSpec((1, L, CONV_CT), lambda c, b: (b, 0, which * nct + c))
    spectra = lambda o: pl.BlockSpec((n, 2 * P, CONV_CT), lambda c, b: (0, 0, o * nct + c),
                                     pipeline_mode=pl.Buffered(1))
    return pl.pallas_call(
        functools.partial(_conv_kernel, P=P, n=n),
        grid=(nct, B),
        in_specs=[col(0), col(1), col(2), spectra(0), spectra(0), spectra(1), spectra(1),
                  pl.BlockSpec((HY_ORDER, CONV_CT), lambda c, b: (0, c)),
                  _const_spec(fwd.shape), _const_spec(inv.shape)],
        out_specs=pl.BlockSpec((1, L, CONV_CT), lambda c, b: (b, 0, c)),
        out_shape=jax.ShapeDtypeStruct((B, L, HY_W), BF16),
        scratch_shapes=[pltpu.VMEM((n, 2 * P, CONV_CT), BF16)] * 2 + [pltpu.VMEM((L, CONV_CT), BF16)],
        compiler_params=_params("arbitrary", "arbitrary"),
        name="long_conv",
    )(hy, hy, hy, gpos, gneg, gpos, gneg, bias, fwd, inv)


def _ret_kernel(q_ref, k_ref, v_ref, g_ref, dm_ref, qdf_ref, kdf_ref, qdb_ref, kdb_ref, cdf_ref, cdb_ref,
                avg_ref, gng_ref, gnb_ref, o_ref, acc_scr, kv_scr, sb_scr, *, L):
    C = RET_CHUNK
    pairs = RET_LANES // LANES
    nchunk = L // C
    lane = lax.broadcasted_iota(jnp.int32, (1, LANES), 1)
    head_a = lane < RT_HEAD_DIM
    ri = lax.broadcasted_iota(jnp.int32, (LANES, LANES), 0)
    ci = lax.broadcasted_iota(jnp.int32, (LANES, LANES), 1)
    same_head = ((ri < RT_HEAD_DIM) == (ci < RT_HEAD_DIM)).astype(F32)
    zero = jnp.zeros((), BF16)

    def split_heads(a):
        return jnp.concatenate([jnp.where(head_a, a, zero), jnp.where(head_a, zero, a)], axis=0)

    def kv_body(nidx, carry):
        rows = pl.ds(pl.multiple_of(nidx * C, C), C)
        for p in range(pairs):
            ls = slice(p * LANES, (p + 1) * LANES)
            kc, vc = k_ref[0, rows, ls].astype(F32), v_ref[0, rows, ls]
            kt = jnp.concatenate([(kc * kdf_ref[p]).T, (kc * kdb_ref[p]).T], axis=0).astype(BF16)
            kv_scr[nidx, p] = _dot(kt, vc) * jnp.concatenate([same_head, same_head], axis=0)
        return carry

    def bwd_scan(it, states):
        nidx = nchunk - 1 - it
        new = []
        for p in range(pairs):
            sb_scr[nidx, p] = states[p].astype(BF16)
            new.append(states[p] * cdb_ref[p] + kv_scr[nidx, p, pl.ds(LANES, LANES), :])
        return tuple(new)

    def main_body(nidx, states):
        rows = pl.ds(pl.multiple_of(nidx * C, C), C)
        new = []
        for p in range(pairs):
            ls = slice(p * LANES, (p + 1) * LANES)
            qc, kc, vc = q_ref[0, rows, ls], k_ref[0, rows, ls], v_ref[0, rows, ls]
            s = lax.dot_general(split_heads(qc), kc, (((1,), (1,)), ((), ())), preferred_element_type=F32)
            s = (s * dm_ref[p]).astype(BF16)
            qf = (qc.astype(F32) * qdf_ref[p]).astype(BF16)
            qb = (qc.astype(F32) * qdb_ref[p]).astype(BF16)
            lhs = jnp.concatenate([s[:C], s[C:], qf, qb], axis=1)
            rhs = jnp.concatenate([split_heads(vc), states[p].astype(BF16), sb_scr[nidx, p]], axis=0)
            acc_scr[rows, ls] = _dot(lhs, rhs)
            new.append(states[p] * cdf_ref[p] + kv_scr[nidx, p, pl.ds(0, LANES), :])
        return tuple(new)

    init = tuple(jnp.zeros((LANES, LANES), F32) for _ in range(pairs))
    lax.fori_loop(0, nchunk, kv_body, 0, unroll=RET_UNROLL)
    lax.fori_loop(0, nchunk, bwd_scan, init)
    lax.fori_loop(0, nchunk, main_body, init, unroll=RET_UNROLL)

    tn = min(ROW_TILE, L)
    avg = avg_ref[...]

    def group_mean(a):
        hi = a.astype(BF16)
        lo = (a - hi.astype(F32)).astype(BF16)
        return _dot(hi, avg) + _dot(lo, avg)

    def norm_body(it, carry):
        rows = pl.ds(pl.multiple_of(it * tn, tn), tn)
        o = acc_scr[rows, :]
        oc = o - group_mean(o)
        var = _dot((oc * oc).astype(BF16), avg)
        on = oc * lax.rsqrt(var + EPS) * gng_ref[...] + gnb_ref[...]
        g = g_ref[0, rows, :].astype(F32)
        o_ref[0, rows, :] = (on * g * (0.5 + 0.5 * jnp.tanh(0.5 * g))).astype(BF16)
        return carry

    lax.fori_loop(0, L // tn, norm_body, 0, unroll=2)


def _retention(rt, tables, gn_g, gn_b):
    B, L, _ = rt.shape
    groups = RT_W // RET_LANES
    pairs = RET_LANES // LANES
    li = jnp.arange(RET_LANES) // RT_HEAD_DIM
    avg = ((li[:, None] == li[None, :]).astype(F32) / RT_HEAD_DIM).astype(BF16)
    col = lambda which: pl.BlockSpec((1, L, RET_LANES), lambda g, b: (b, 0, which * groups + g))
    tab = lambda a: pl.BlockSpec((pairs,) + a.shape[1:], lambda g, b: (g, 0, 0))
    vec = pl.BlockSpec((1, RET_LANES), lambda g, b: (0, g))
    return pl.pallas_call(
        functools.partial(_ret_kernel, L=L),
        grid=(groups, B),
        in_specs=[col(0), col(1), col(2), col(3)] + [tab(a) for a in tables] + [_const_spec(avg.shape), vec, vec],
        out_specs=pl.BlockSpec((1, L, RET_LANES), lambda g, b: (b, 0, g)),
        out_shape=jax.ShapeDtypeStruct((B, L, RT_W), BF16),
        scratch_shapes=[pltpu.VMEM((L, RET_LANES), F32),
                        pltpu.VMEM((L // RET_CHUNK, pairs, 2 * LANES, LANES), F32),
                        pltpu.VMEM((L // RET_CHUNK, pairs, LANES, LANES), BF16)],
        compiler_params=_params("arbitrary", "arbitrary"),
        name="retention",
    )(rt, rt, rt, rt, *tables, avg, gn_g.reshape(1, -1), gn_b.reshape(1, -1))


def _gelu_tanh(a):
    return 0.5 * a * (1.0 + jnp.tanh(math.sqrt(2.0 / math.pi) * (a + 0.044715 * (a * a * a))))


def _rms(x, g):
    return x * lax.rsqrt(jnp.mean(x * x, axis=-1, keepdims=True) + EPS) * g


def _ffn_kernel(xp_ref, xm_ref, xn_ref, hp_ref, hm_ref, hn_ref, rp_ref, rm_ref, rn_ref, hg_ref, woh_ref, wor_ref,
                g2_ref, wa_ref, wb_ref, cwa_ref, cwb_ref, cba_ref, cbb_ref, wd_ref, gf_ref, o_ref,
                ua_scr, ub_scr, *, tm):
    i = pl.program_id(1)
    last = pl.num_programs(1) - 1
    def cat(p, m, n):
        return jnp.concatenate([p[0].astype(F32)[FHALO - HALO:], m[0].astype(F32), n[0].astype(F32)[:HALO]], axis=0)

    yh = _rms(cat(hp_ref, hm_ref, hn_ref), hg_ref[...]).astype(BF16)
    x1 = (cat(xp_ref, xm_ref, xn_ref) + _dot(yh, woh_ref[...])
          + _dot(cat(rp_ref, rm_ref, rn_ref).astype(BF16), wor_ref[...]))
    xe = _rms(x1, g2_ref[...])
    row = lax.broadcasted_iota(jnp.int32, (tm + 2 * HALO, 1), 0)
    outside = ((row < HALO) & (i == 0)) | ((row >= tm + HALO) & (i == last))
    xe = jnp.where(outside, 0.0, xe).astype(BF16)

    def conv3(scr, cw_ref, cb_ref):
        return (scr[pl.ds(HALO - 1, tm), :] * cw_ref[0:1, :] + scr[pl.ds(HALO, tm), :] * cw_ref[1:2, :]
                + scr[pl.ds(HALO + 1, tm), :] * cw_ref[2:3, :] + cb_ref[...])

    ua_scr[...] = _dot(xe, wa_ref[...])
    ub_scr[...] = _dot(xe, wb_ref[...])
    gated = _gelu_tanh(conv3(ua_scr, cwa_ref, cba_ref)) * conv3(ub_scr, cwb_ref, cbb_ref)
    y = x1[HALO:HALO + tm] + _dot(gated.astype(BF16), wd_ref[...])
    o_ref[0] = _rms(y, gf_ref[...])


def _mix_ffn(x, zh, zr, hy_out_g, w_out, norm2_g, w_up, conv_w, conv_b, w_down, final_g):
    B, L, D = x.shape
    tm = min(ROW_TILE, L)
    blocks = tm // FHALO

    def halo3(w):
        prev = pl.BlockSpec((1, FHALO, w), lambda b, i: (b, jnp.maximum(i * blocks - 1, 0), 0))
        main = pl.BlockSpec((1, tm, w), lambda b, i: (b, i, 0))
        nxt = pl.BlockSpec((1, FHALO, w), lambda b, i: (b, jnp.minimum((i + 1) * blocks, L // FHALO - 1), 0))
        return [prev, main, nxt]

    consts = (hy_out_g.reshape(1, -1), w_out[:HY_W].astype(BF16), w_out[HY_W:].astype(BF16),
              norm2_g.reshape(1, D), w_up[:, :D_FF].astype(BF16), w_up[:, D_FF:].astype(BF16),
              conv_w[:, :D_FF], conv_w[:, D_FF:], conv_b[:D_FF].reshape(1, -1), conv_b[D_FF:].reshape(1, -1),
              w_down.astype(BF16), final_g.reshape(1, D))
    return pl.pallas_call(
        functools.partial(_ffn_kernel, tm=tm),
        grid=(B, L // tm),
        in_specs=halo3(D) + halo3(HY_W) + halo3(RT_W) + [_const_spec(a.shape) for a in consts],
        out_specs=pl.BlockSpec((1, tm, D), lambda b, i: (b, i, 0)),
        out_shape=jax.ShapeDtypeStruct((B, L, D), F32),
        scratch_shapes=[pltpu.VMEM((tm + 2 * HALO, D_FF), F32)] * 2,
        compiler_params=_params("parallel", "arbitrary"),
        name="mix_ffn",
    )(x, x, x, zh, zh, zh, zr, zr, zr, *consts)


def _trunk(x, prm, spectra, fwd, inv, rot, ret_tables, n):
    hy, rt = _in_proj(x, prm["norm1_g"], prm["w_in"], prm["hy_conv_w"], prm["hy_conv_b"], *rot)
    z = _long_conv(hy, *spectra, prm["hy_bias"], fwd, inv, n)
    zr = _retention(rt, ret_tables, prm["ret_gn_g"], prm["ret_gn_b"])
    return _mix_ffn(x, z, zr, prm["hy_out_g"], prm["w_out"], prm["norm2_g"], prm["w_up"], prm["ffn_conv_w"],
                    prm["ffn_conv_b"], prm["w_down"], prm["final_g"])


def _layer(xs, prm, n=CONV_BLOCKS):
    L = xs[0].shape[1]
    assert all(x.shape[1] == L for x in xs) and L % (n * CMAC_ROWS) == 0 and L % (RET_CHUNK * RET_UNROLL) == 0
    fwd, inv = _dft_tables(L // n)
    spectra = _filter_spectra(L, n, fwd, prm["filt_w1"], prm["filt_b1"], prm["filt_w2"], prm["filt_b2"],
                              prm["filt_w3"], prm["filt_b3"], prm["filt_freq"], prm["filt_w4"])
    rot = _rotary_tables(L)
    ret_tables = _retention_tables()
    return tuple(_trunk(x, prm, spectra, fwd, inv, rot, ret_tables, n) for x in xs)


def kernel(x_prompt, x_sample, norm1_g, w_in, hy_conv_w, hy_conv_b, filt_w1, filt_b1, filt_w2, filt_b2,
           filt_w3, filt_b3, filt_freq, filt_w4, hy_bias, hy_out_g, ret_gn_g, ret_gn_b, w_out, norm2_g,
           w_up, ffn_conv_w, ffn_conv_b, w_down, final_g):
    layer = dict(norm1_g=norm1_g, w_in=w_in, hy_conv_w=hy_conv_w, hy_conv_b=hy_conv_b, filt_w1=filt_w1,
                 filt_b1=filt_b1, filt_w2=filt_w2, filt_b2=filt_b2, filt_w3=filt_w3, filt_b3=filt_b3,
                 filt_freq=filt_freq, filt_w4=filt_w4, hy_bias=hy_bias, hy_out_g=hy_out_g,
                 ret_gn_g=ret_gn_g, ret_gn_b=ret_gn_b, w_out=w_out, norm2_g=norm2_g, w_up=w_up,
                 ffn_conv_w=ffn_conv_w, ffn_conv_b=ffn_conv_b, w_down=w_down)
    assert all(v.shape[0] == 1 for v in layer.values())
    prm = {k: v[0] for k, v in layer.items()}
    prm["final_g"] = final_g
    return _layer((x_prompt, x_sample), prm)
```
